```python
import math
import jax, jax.numpy as jnp
from jax import lax
import numpy as np

D_MODEL = 2048
BATCH = 32
SEQ = 256
DEPTH = 4
DEC_BATCH = 8
DEC_SEQ = 2048
PAST_LEN = 512

GRID_W = 64
WIN_R = 8
WIN_C = 16
Q_BLOCK = 128
EPS = 1e-6
ROPE_BASE = 10000.0
N_EVEN = (DEPTH + 1) // 2
N_ODD = DEPTH // 2
NA_DIM = 128
NA_WIDTH = D_MODEL // 2
NA_HEADS = NA_WIDTH // NA_DIM
MLA_NOPE = 128
MLA_ROPE = 64
MLA_V = 128
MLA_WIDTH = D_MODEL // 2
MLA_HEADS = MLA_WIDTH // MLA_V
MLA_Q_LORA = D_MODEL // 4
MLA_KV_LORA = D_MODEL // 8
EVEN_SPLITS = (NA_WIDTH, NA_WIDTH, NA_WIDTH, NA_WIDTH, MLA_Q_LORA, MLA_KV_LORA, MLA_ROPE, MLA_WIDTH)
EVEN_IN = 4 * NA_WIDTH + MLA_Q_LORA + MLA_KV_LORA + MLA_ROPE + MLA_WIDTH
DIFF_HEADS = 8
DIFF_D = D_MODEL // (2 * DIFF_HEADS)
DIFF_WIDTH = 2 * DIFF_HEADS * DIFF_D
ODD_IN = 4 * DIFF_WIDTH
NA_SCALE = NA_DIM ** -0.5
MLA_SCALE = (MLA_NOPE + MLA_ROPE) ** -0.5
DIFF_SCALE = DIFF_D ** -0.5

kernel_name = 'hybrid_diffusion_na_mla_diffattn_step'


def _split(x, sizes):
    out, start = [], 0
    for s in sizes:
        out.append(x[..., start:start + s])
        start += s
    return out


def rmsnorm(x, g):
    x32 = x.astype(jnp.float32)
    y = x32 * lax.rsqrt(jnp.mean(x32 * x32, axis=-1, keepdims=True) + EPS)
    return (y * g.astype(jnp.float32)).astype(x.dtype)


def modulation(cond, w_ada, b_ada):
    ada = (jax.nn.silu(cond) @ w_ada + b_ada)[:, None, :]
    return jnp.split(ada, 3, axis=-1)


def rope_1d(x, pos):
    half = x.shape[-1] // 2
    inv = ROPE_BASE ** (-jnp.arange(half, dtype=jnp.float32) / half)
    ang = pos.astype(jnp.float32)[:, None] * inv[None, :]
    cos = jnp.cos(ang)[:, None, :]
    sin = jnp.sin(ang)[:, None, :]
    x1 = x[..., :half].astype(jnp.float32)
    x2 = x[..., half:].astype(jnp.float32)
    return jnp.concatenate([x1 * cos - x2 * sin, x2 * cos + x1 * sin], axis=-1).astype(x.dtype)


def axial_rope(x, rows, cols):
    a = x.shape[-1] // 2
    return jnp.concatenate([rope_1d(x[..., :a], rows), rope_1d(x[..., a:], cols)], axis=-1)


def dense_attention(q, k, v, scale):
    B, Sq, H, d = q.shape
    nb = Sq // Q_BLOCK
    qb = q.reshape(B, nb, Q_BLOCK, H, d).transpose(1, 0, 2, 3, 4)

    def block(q_blk):
        s = jnp.einsum('bqhd,bkhd->bhqk', q_blk, k).astype(jnp.float32) * scale
        p = jax.nn.softmax(s, axis=-1).astype(v.dtype)
        return jnp.einsum('bhqk,bkhd->bqhd', p, v)

    o = lax.map(block, qb)
    return o.transpose(1, 0, 2, 3, 4).reshape(B, Sq, H, v.shape[-1])


def na_tables(rows):
    wr = min(WIN_R, rows)
    r = np.arange(rows)
    c = np.arange(GRID_W)
    rs = np.clip(r - wr // 2, 0, rows - wr)
    cs = np.clip(c - WIN_C // 2, 0, GRID_W - WIN_C)
    kr = rs[:, None] + np.arange(wr)[None, :]
    kc = cs[:, None] + np.arange(WIN_C)[None, :]
    key_idx = kr[:, None, :, None] * GRID_W + kc[None, :, None, :]
    dr = kr - r[:, None]
    dc = kc - c[:, None]
    bias_idx = (dr[:, None, :, None] + WIN_R - 1) * (2 * WIN_C - 1) + (dc[None, :, None, :] + WIN_C - 1)
    K = wr * WIN_C
    return (jnp.asarray(key_idx.reshape(rows, GRID_W, K), dtype=jnp.int32),
            jnp.asarray(bias_idx.reshape(rows, GRID_W, K), dtype=jnp.int32))


def na_latent(q, k, v, k_ctx, v_ctx, rpb):
    B, S, H, d = q.shape
    rows = S // GRID_W
    key_idx, bias_idx = na_tables(rows)
    rpb_flat = rpb.reshape(H, -1)
    qr = q.reshape(B, rows, GRID_W, H, d).transpose(1, 0, 2, 3, 4)

    def row_block(args):
        q_blk, kidx, bidx = args
        k_win = k[:, kidx]
        v_win = v[:, kidx]
        bias = rpb_flat[:, bidx].astype(jnp.float32)
        s_win = jnp.einsum('bqhd,bqkhd->bhqk', q_blk, k_win).astype(jnp.float32) * NA_SCALE + bias[None]
        s_ctx = jnp.einsum('bqhd,bkhd->bhqk', q_blk, k_ctx).astype(jnp.float32) * NA_SCALE
        p = jax.nn.softmax(jnp.concatenate([s_win, s_ctx], axis=-1), axis=-1).astype(v.dtype)
        K = kidx.shape[-1]
        return (jnp.einsum('bhqk,bqkhd->bqhd', p[..., :K], v_win)
                + jnp.einsum('bhqk,bkhd->bqhd', p[..., K:], v_ctx))

    o = lax.map(row_block, (qr, key_idx, bias_idx))
    return o.transpose(1, 0, 2, 3, 4).reshape(B, S, H, d)


def even_project(h, w_in, g_q, w_uq, g_kv):
    B, S, _ = h.shape
    qa, ka, va, ga, cq, ckv_raw, kpe, gb = _split(h @ w_in, EVEN_SPLITS)
    heads = lambda t: t.reshape(B, S, NA_HEADS, NA_DIM)
    q_mla = (rmsnorm(cq, g_q) @ w_uq).reshape(B, S, MLA_HEADS, MLA_NOPE + MLA_ROPE)
    ckv = rmsnorm(ckv_raw, g_kv)
    return heads(qa), heads(ka), heads(va), ga, q_mla, ckv, kpe, gb


def mla_kv(ckv, kpe, w_ukv):
    B, S, _ = ckv.shape
    kv = (ckv @ w_ukv).reshape(B, S, MLA_HEADS, MLA_NOPE + MLA_V)
    k = jnp.concatenate([kv[..., :MLA_NOPE],
                         jnp.broadcast_to(kpe[:, :, None, :], (B, S, MLA_HEADS, MLA_ROPE))], axis=-1)
    return k, kv[..., MLA_NOPE:]


def even_merge(oa, ga, ob, gb, w_out):
    B, S = ga.shape[:2]
    y = jnp.concatenate([oa.reshape(B, S, -1) * jax.nn.silu(ga),
                         ob.reshape(B, S, -1) * jax.nn.silu(gb)], axis=-1)
    return y @ w_out


def even_context(h, w_in, w_out, g_q, w_uq, g_kv, w_ukv):
    qa, ka, va, ga, q_mla, ckv, kpe, gb = even_project(h, w_in, g_q, w_uq, g_kv)
    oa = dense_attention(qa, ka, va, NA_SCALE)
    kb, vb = mla_kv(ckv, kpe, w_ukv)
    ob = dense_attention(q_mla, kb, vb, MLA_SCALE)
    return even_merge(oa, ga, ob, gb, w_out), (ka, va, ckv, kpe)


def even_latent(h, ctx_k, ctx_v, ctx_ckv, ctx_kpe, rows, cols, w_in, w_out, rpb, g_q, w_uq, g_kv, w_ukv):
    qa, ka, va, ga, q_mla, ckv, kpe, gb = even_project(h, w_in, g_q, w_uq, g_kv)
    oa = na_latent(qa, ka, va, ctx_k, ctx_v, rpb)
    q_mla = jnp.concatenate([q_mla[..., :MLA_NOPE], axial_rope(q_mla[..., MLA_NOPE:], rows, cols)], axis=-1)
    kpe = axial_rope(kpe[:, :, None, :], rows, cols)[:, :, 0, :]
    kb_lat, vb_lat = mla_kv(ckv, kpe, w_ukv)
    kb_ctx, vb_ctx = mla_kv(ctx_ckv, ctx_kpe, w_ukv)
    ob = dense_attention(q_mla, jnp.concatenate([kb_ctx, kb_lat], axis=1),
                         jnp.concatenate([vb_ctx, vb_lat], axis=1), MLA_SCALE)
    return even_merge(oa, ga, ob, gb, w_out)


def odd_project(h, w_in):
    B, S, _ = h.shape
    q, k, v, g = jnp.split(h @ w_in, 4, axis=-1)
    q = q.reshape(B, S, DIFF_HEADS, 2, DIFF_D)
    k = k.reshape(B, S, DIFF_HEADS, 2, DIFF_D)
    v = v.reshape(B, S, DIFF_HEADS, 2 * DIFF_D)
    return q, k, v, g


def diff_rope(t, rows, cols):
    B, S = t.shape[:2]
    return axial_rope(t.reshape(B, S, 2 * DIFF_HEADS, DIFF_D), rows, cols).reshape(B, S, DIFF_HEADS, 2, DIFF_D)


def diff_attend(q, k, v, g, lam_params, g_sub, lam_init, w_out):
    B, S = q.shape[:2]
    o1 = dense_attention(q[..., 0, :], k[..., 0, :], v, DIFF_SCALE)
    o2 = dense_attention(q[..., 1, :], k[..., 1, :], v, DIFF_SCALE)
    lp = lam_params.astype(jnp.float32)
    lam = jnp.exp(jnp.sum(lp[0] * lp[1])) - jnp.exp(jnp.sum(lp[2] * lp[3])) + lam_init
    o = rmsnorm(o1 - lam.astype(o1.dtype) * o2, g_sub) * (1.0 - lam_init)
    return (o.reshape(B, S, -1) * jax.nn.silu(g)) @ w_out


def odd_context(h, w_in, w_out, lam_params, g_sub, lam_init):
    B, S, _ = h.shape
    q, k, v, g = odd_project(h, w_in)
    y = diff_attend(q, k, v, g, lam_params, g_sub, lam_init, w_out)
    return y, (k.reshape(B, S, DIFF_HEADS, 2 * DIFF_D), v)


def odd_latent(h, ctx_k, ctx_v, rows, cols, w_in, w_out, lam_params, g_sub, lam_init):
    B, L = ctx_k.shape[:2]
    q, k, v, g = odd_project(h, w_in)
    q = diff_rope(q, rows, cols)
    k = diff_rope(k, rows, cols)
    k_all = jnp.concatenate([ctx_k.reshape(B, L, DIFF_HEADS, 2, DIFF_D), k], axis=1)
    v_all = jnp.concatenate([ctx_v, v], axis=1)
    return diff_attend(q, k_all, v_all, g, lam_params, g_sub, lam_init, w_out)


def setup_inputs(seed: int = 0) -> dict:
    key = jax.random.key(seed)
    ks = jax.random.split(key, 32)
    nrm = lambda k, shape, s: jax.random.normal(k, shape, jnp.float32) * s
    D = D_MODEL
    return {
        'x_prompt': nrm(ks[0], (BATCH, SEQ, D), 1.0),
        'x_sample': nrm(ks[1], (DEC_BATCH, DEC_SEQ, D), 1.0),
        'cache_na_k': nrm(ks[2], (DEC_BATCH, N_EVEN, PAST_LEN, NA_HEADS, NA_DIM), 1.0),
        'cache_na_v': nrm(ks[3], (DEC_BATCH, N_EVEN, PAST_LEN, NA_HEADS, NA_DIM), 1.0),
        'cache_mla_ckv': nrm(ks[4], (DEC_BATCH, N_EVEN, PAST_LEN, MLA_KV_LORA), 1.0),
        'cache_mla_kpe': nrm(ks[5], (DEC_BATCH, N_EVEN, PAST_LEN, MLA_ROPE), 1.0),
        'cache_diff_k': nrm(ks[6], (DEC_BATCH, N_ODD, PAST_LEN, DIFF_HEADS, 2 * DIFF_D), 1.0),
        'cache_diff_v': nrm(ks[7], (DEC_BATCH, N_ODD, PAST_LEN, DIFF_HEADS, 2 * DIFF_D), 1.0),
        'c': nrm(ks[8], (DEC_BATCH, D), 1.0),
        'c_ctx': nrm(ks[9], (D,), 1.0),
        'w_ada': nrm(ks[10], (DEPTH, D, 3 * D), 0.5 * D ** -0.5),
        'b_ada': nrm(ks[11], (DEPTH, 3 * D), 0.01),
        'g_pre': 1.0 + nrm(ks[12], (DEPTH, D), 0.01),
        'g_post': 1.0 + nrm(ks[13], (DEPTH, D), 0.01),
        'w_in_even': nrm(ks[14], (N_EVEN, D, EVEN_IN), D ** -0.5),
        'w_out_even': nrm(ks[15], (N_EVEN, NA_WIDTH + MLA_WIDTH, D), (NA_WIDTH + MLA_WIDTH) ** -0.5),
        'na_rpb': nrm(ks[16], (N_EVEN, NA_HEADS, 2 * WIN_R - 1, 2 * WIN_C - 1), 0.1),
        'mla_g_q': 1.0 + nrm(ks[17], (N_EVEN, MLA_Q_LORA), 0.01),
        'mla_w_uq': nrm(ks[18], (N_EVEN, MLA_Q_LORA, MLA_HEADS * (MLA_NOPE + MLA_ROPE)), MLA_Q_LORA ** -0.5),
        'mla_g_kv': 1.0 + nrm(ks[19], (N_EVEN, MLA_KV_LORA), 0.01),
        'mla_w_ukv': nrm(ks[20], (N_EVEN, MLA_KV_LORA, MLA_HEADS * (MLA_NOPE + MLA_V)), MLA_KV_LORA ** -0.5),
        'w_in_odd': nrm(ks[21], (N_ODD, D, ODD_IN), D ** -0.5),
        'w_out_odd': nrm(ks[22], (N_ODD, DIFF_WIDTH, D), DIFF_WIDTH ** -0.5),
        'diff_lambda': nrm(ks[23], (N_ODD, 4, DIFF_D), 0.1),
        'diff_g': 1.0 + nrm(ks[24], (N_ODD, 2 * DIFF_D), 0.01),
    }


def reference(x_prompt, x_sample, cache_na_k, cache_na_v, cache_mla_ckv, cache_mla_kpe, cache_diff_k,
              cache_diff_v, c, c_ctx, w_ada, b_ada, g_pre, g_post, w_in_even, w_out_even, na_rpb,
              mla_g_q, mla_w_uq, mla_g_kv, mla_w_ukv, w_in_odd, w_out_odd, diff_lambda, diff_g):
    S = x_sample.shape[1]
    t = jnp.arange(S)
    rows = t // GRID_W
    cols = t % GRID_W
    xp, xs = x_prompt, x_sample
    na_k, na_v, mla_ckv, mla_kpe, diff_k, diff_v = [], [], [], [], [], []
    for l in range(DEPTH):
        i = l // 2
        sh_p, sc_p, gt_p = modulation(c_ctx[None, :], w_ada[l], b_ada[l])
        sh_s, sc_s, gt_s = modulation(c, w_ada[l], b_ada[l])
        hp = rmsnorm(xp, g_pre[l]) * (1.0 + sc_p) + sh_p
        hs = rmsnorm(xs, g_pre[l]) * (1.0 + sc_s) + sh_s
        if l % 2 == 0:
            yp, (ka, va, ckv, kpe) = even_context(hp, w_in_even[i], w_out_even[i], mla_g_q[i], mla_w_uq[i],
                                                  mla_g_kv[i], mla_w_ukv[i])
            na_k.append(ka)
            na_v.append(va)
            mla_ckv.append(ckv)
            mla_kpe.append(kpe)
            ys = even_latent(hs, cache_na_k[:, i], cache_na_v[:, i], cache_mla_ckv[:, i], cache_mla_kpe[:, i],
                             rows, cols, w_in_even[i], w_out_even[i], na_rpb[i], mla_g_q[i], mla_w_uq[i],
                             mla_g_kv[i], mla_w_ukv[i])
        else:
            lam_init = 0.8 - 0.6 * math.exp(-0.3 * l)
            yp, (kd, vd) = odd_context(hp, w_in_odd[i], w_out_odd[i], diff_lambda[i], diff_g[i], lam_init)
            diff_k.append(kd)
            diff_v.append(vd)
            ys = odd_latent(hs, cache_diff_k[:, i], cache_diff_v[:, i], rows, cols, w_in_odd[i], w_out_odd[i],
                            diff_lambda[i], diff_g[i], lam_init)
        xp = xp + gt_p * rmsnorm(yp, g_post[l])
        xs = xs + gt_s * rmsnorm(ys, g_post[l])
    return (xp, xs, jnp.stack(na_k, axis=1), jnp.stack(na_v, axis=1), jnp.stack(mla_ckv, axis=1),
            jnp.stack(mla_kpe, axis=1), jnp.stack(diff_k, axis=1), jnp.stack(diff_v, axis=1))
```

```python
import functools
import math

import numpy as np
import jax
import jax.numpy as jnp
from jax import lax
from jax.experimental import pallas as pl
from jax.experimental.pallas import tpu as pltpu

GRID_W = 64
WIN_R = 8
WIN_C = 16
EPS = 1e-6
ROPE_BASE = 10000.0
LOG2E = 1.4426950408889634
NEG_BIG = -1e30

LANES = 128
VMEM_LIMIT = 56 * 1024 * 1024

F32 = jnp.float32
BF16 = jnp.bfloat16


def _params(sem, vmem=VMEM_LIMIT):
    return pltpu.CompilerParams(dimension_semantics=sem, vmem_limit_bytes=vmem)


def _rms(x, g):
    return x * lax.rsqrt(jnp.mean(x * x, axis=-1, keepdims=True) + EPS) * g


def _silu(x):
    return x * (1.0 / (1.0 + jnp.exp(-x)))


def _rope(x, cos, sin_lo, sin_hi, half):
    return (x * cos + pltpu.roll(x, LANES - half, 1) * sin_lo
            + pltpu.roll(x, half, 1) * sin_hi)


def _ada_kernel(cond_ref, w_ref, b_ref, o_ref):
    a = _silu(cond_ref[...]).astype(BF16)
    o_ref[...] = jnp.dot(a, w_ref[...].astype(BF16), preferred_element_type=F32) + b_ref[...]


def _ada(cond, w_ada, b_ada):
    depth, d, n = w_ada.shape
    rows = cond.shape[0]
    tn = 1024
    return pl.pallas_call(
        _ada_kernel,
        out_shape=jax.ShapeDtypeStruct((depth, rows, n), F32),
        grid=(depth, n // tn),
        in_specs=[
            pl.BlockSpec((rows, d), lambda l, j: (0, 0)),
            pl.BlockSpec((None, d, tn), lambda l, j: (l, 0, j)),
            pl.BlockSpec((None, 1, tn), lambda l, j: (l, 0, j)),
        ],
        out_specs=pl.BlockSpec((None, rows, tn), lambda l, j: (l, 0, j)),
        compiler_params=_params(("arbitrary", "arbitrary")),
        name="ada",
    )(cond, w_ada, b_ada.reshape(depth, 1, n))


def _proj_kernel(*refs, n_groups, rope_flags, rope_half):
    x_ref, g_ref, sh_ref, sc_ref = refs[:4]
    pos = 4
    if any(rope_flags):
        cos_ref, slo_ref, shi_ref = refs[pos:pos + 3]
        pos += 3
    w_refs = refs[pos:pos + n_groups]
    o_refs = refs[pos + n_groups:pos + 2 * n_groups]
    h_ref = refs[pos + 2 * n_groups]

    @pl.when(pl.program_id(1) == 0)
    def _():
        h = _rms(x_ref[...], g_ref[...]) * (1.0 + sc_ref[...]) + sh_ref[...]
        h_ref[...] = h.astype(BF16)

    h = h_ref[...]
    for gi in range(n_groups):
        acc = jnp.dot(h, w_refs[gi][...], preferred_element_type=F32)
        if rope_flags[gi]:
            cos, slo, shi = cos_ref[...], slo_ref[...], shi_ref[...]
            parts = [_rope(acc[:, s:s + LANES], cos, slo, shi, rope_half)
                     for s in range(0, acc.shape[1], LANES)]
            acc = jnp.concatenate(parts, axis=1)
        o_refs[gi][...] = acc.astype(o_refs[gi].dtype)


def _proj(x, seq, g_pre, ada_l, mod_row, weights, out_dtypes, steps, rope_flags=None,
          rope_tables=None, rope_half=0, tm=512):
    t, d = x.shape
    n_groups = len(weights)
    rope_flags = tuple(rope_flags) if rope_flags is not None else (False,) * n_groups
    tiles_per_seq = max(seq // tm, 1)
    in_specs = [
        pl.BlockSpec((tm, d), lambda i, j: (i, 0)),
        pl.BlockSpec((1, d), lambda i, j: (0, 0)),
        pl.BlockSpec((None, 1, d), lambda i, j: (mod_row(i), 0, 0)),
        pl.BlockSpec((None, 1, d), lambda i, j: (mod_row(i), 0, 1)),
    ]
    args = [x, g_pre.reshape(1, d), ada_l, ada_l]
    if any(rope_flags):
        for tab in rope_tables:
            in_specs.append(pl.BlockSpec((tm, LANES), lambda i, j: (i % tiles_per_seq, 0)))
            args.append(tab)
    out_specs, out_shapes = [], []
    for w, dt in zip(weights, out_dtypes):
        n = w.shape[1]
        tn = n // steps
        in_specs.append(pl.BlockSpec((d, tn), lambda i, j: (0, j)))
        args.append(w)
        out_specs.append(pl.BlockSpec((tm, tn), lambda i, j: (i, j)))
        out_shapes.append(jax.ShapeDtypeStruct((t, n), dt))
    return pl.pallas_call(
        functools.partial(_proj_kernel, n_groups=n_groups, rope_flags=rope_flags, rope_half=rope_half),
        out_shape=out_shapes,
        grid=(t // tm, steps),
        in_specs=in_specs,
        out_specs=out_specs,
        scratch_shapes=[pltpu.VMEM((tm, d), BF16)],
        compiler_params=_params(("arbitrary", "arbitrary")),
        name="proj",
    )(*args)


def _mla_q_kernel(*refs, rope, rope_half):
    if rope:
        cq_ref, g_ref, w_ref, cos_ref, slo_ref, shi_ref, o_ref = refs
    else:
        cq_ref, g_ref, w_ref, o_ref = refs
    n = _rms(cq_ref[...], g_ref[...]).astype(BF16)
    q = jnp.dot(n, w_ref[...], preferred_element_type=F32)
    if rope:
        cos, slo, shi = cos_ref[...], slo_ref[...], shi_ref[...]
        parts = []
        for s in range(0, q.shape[1], 2 * LANES):
            parts.append(q[:, s:s + LANES])
            parts.append(_rope(q[:, s + LANES:s + 2 * LANES], cos, slo, shi, rope_half))
        q = jnp.concatenate(parts, axis=1)
    o_ref[...] = q.astype(o_ref.dtype)


def _mla_q(cq, seq, g_q, w_uq_cat, rope_tables=None, rope_half=0, tm=512):
    t, r = cq.shape
    n = w_uq_cat.shape[1]
    rope = rope_tables is not None
    tiles_per_seq = max(seq // tm, 1)
    in_specs = [
        pl.BlockSpec((tm, r), lambda i: (i, 0)),
        pl.BlockSpec((1, r), lambda i: (0, 0)),
        pl.BlockSpec((r, n), lambda i: (0, 0)),
    ]
    args = [cq, g_q.reshape(1, r), w_uq_cat]
    if rope:
        for tab in rope_tables:
            in_specs.append(pl.BlockSpec((tm, LANES), lambda i: (i % tiles_per_seq, 0)))
            args.append(tab)
    return pl.pallas_call(
        functools.partial(_mla_q_kernel, rope=rope, rope_half=rope_half),
        out_shape=jax.ShapeDtypeStruct((t, n), BF16),
        grid=(t // tm,),
        in_specs=in_specs,
        out_specs=pl.BlockSpec((tm, n), lambda i: (i, 0)),
        compiler_params=_params(("arbitrary",)),
        name="mla_q",
    )(*args)


def _mla_kv_kernel(*refs, normalise, rope, rope_half, emit_cache, n_heads, nope):
    refs = list(refs)
    ckv_ref, kpe_ref = refs[:2]
    pos = 2
    if normalise:
        g_ref = refs[pos]
        pos += 1
    wk_ref, wv_ref = refs[pos:pos + 2]
    pos += 2
    if rope:
        cos_ref, slo_ref, shi_ref = refs[pos:pos + 3]
        pos += 3
    kcat_ref, v_ref = refs[pos:pos + 2]
    pos += 2
    ckv = ckv_ref[...]
    if normalise:
        ckv = _rms(ckv, g_ref[...])
    kpe = kpe_ref[...]
    if emit_cache:
        ckv_out_ref, kpe_out_ref = refs[pos:pos + 2]
        ckv_out_ref[...] = ckv
        kpe_out_ref[...] = kpe[:, :kpe_out_ref.shape[1]]
    if rope:
        kpe = _rope(kpe, cos_ref[...], slo_ref[...], shi_ref[...], rope_half)
    cb = ckv.astype(BF16)
    kn = jnp.dot(cb, wk_ref[...], preferred_element_type=F32).astype(BF16)
    v_ref[...] = jnp.dot(cb, wv_ref[...], preferred_element_type=F32).astype(BF16)
    kpe_b = kpe.astype(BF16)
    for h in range(n_heads):
        kcat_ref[:, h * 2 * nope:h * 2 * nope + nope] = kn[:, h * nope:(h + 1) * nope]
        kcat_ref[:, h * 2 * nope + nope:(h + 1) * 2 * nope] = kpe_b


def _mla_kv(ckv_arr, ckv_spec, kpe_arr, kpe_spec, t, seq, w_uk, w_uv, g_kv=None, rope_tables=None,
            rope_half=0, emit_cache=False, kpe_width=64, tm=512):
    lora, nk = w_uk.shape
    nv = w_uv.shape[1]
    nope = LANES
    n_heads = nk // nope
    normalise = g_kv is not None
    rope = rope_tables is not None
    tiles_per_seq = max(seq // tm, 1)
    in_specs = [ckv_spec, kpe_spec]
    args = [ckv_arr, kpe_arr]
    if normalise:
        in_specs.append(pl.BlockSpec((1, lora), lambda i: (0, 0)))
        args.append(g_kv.reshape(1, lora))
    in_specs += [pl.BlockSpec((lora, nk), lambda i: (0, 0)), pl.BlockSpec((lora, nv), lambda i: (0, 0))]
    args += [w_uk, w_uv]
    if rope:
        for tab in rope_tables:
            in_specs.append(pl.BlockSpec((tm, LANES), lambda i: (i % tiles_per_seq, 0)))
            args.append(tab)
    out_shapes = [jax.ShapeDtypeStruct((t, 2 * nk), BF16), jax.ShapeDtypeStruct((t, nv), BF16)]
    out_specs = [pl.BlockSpec((tm, 2 * nk), lambda i: (i, 0)), pl.BlockSpec((tm, nv), lambda i: (i, 0))]
    if emit_cache:
        out_shapes += [jax.ShapeDtypeStruct((t, lora), F32), jax.ShapeDtypeStruct((t, kpe_width), F32)]
        out_specs += [pl.BlockSpec((tm, lora), lambda i: (i, 0)), pl.BlockSpec((tm, kpe_width), lambda i: (i, 0))]
    return pl.pallas_call(
        functools.partial(_mla_kv_kernel, normalise=normalise, rope=rope, rope_half=rope_half,
                          emit_cache=emit_cache, n_heads=n_heads, nope=nope),
        out_shape=out_shapes,
        grid=(t // tm,),
        in_specs=in_specs,
        out_specs=out_specs,
        compiler_params=_params(("arbitrary",)),
        name="mla_kv",
    )(*args)


def _softmax_parts(s_list, c):
    m = functools.reduce(jnp.maximum, [jnp.max(s, axis=-1, keepdims=True) for s in s_list])
    p_list = [jnp.exp2((s - m) * c) for s in s_list]
    l = functools.reduce(lambda a, b: a + b, [jnp.sum(p, axis=-1, keepdims=True) for p in p_list])
    return p_list, l


def _qk(q, k):
    return lax.dot_general(q, k, (((1,), (1,)), ((), ())), preferred_element_type=F32)


def _attn_kernel(*refs, n_seg, scale, diff, lam_init):
    q_ref = refs[0]
    kv_refs = refs[1:1 + 2 * n_seg]
    g_ref = refs[1 + 2 * n_seg]
    pos = 2 + 2 * n_seg
    if diff:
        lam_ref, gsub_ref = refs[pos:pos + 2]
        pos += 2
    o_ref = refs[pos]
    c = scale * LOG2E
    q = q_ref[...].astype(BF16)
    ks = [kv_refs[2 * i][...].astype(BF16) for i in range(n_seg)]
    vs = [kv_refs[2 * i + 1][...].astype(BF16) for i in range(n_seg)]
    if not diff:
        p_list, l = _softmax_parts([_qk(q, k) for k in ks], c)
        o = functools.reduce(lambda a, b: a + b,
                             [jnp.dot(p.astype(BF16), v, preferred_element_type=F32)
                              for p, v in zip(p_list, vs)])
        o = o / l
    else:
        d = q.shape[1] // 2
        lp = lam_ref[...]
        lam = (jnp.exp(jnp.sum(lp[0:1] * lp[1:2], axis=-1, keepdims=True))
               - jnp.exp(jnp.sum(lp[2:3] * lp[3:4], axis=-1, keepdims=True)) + lam_init)
        p1, l1 = _softmax_parts([_qk(q[:, :d], k[:, :d]) for k in ks], c)
        p2, l2 = _softmax_parts([_qk(q[:, d:], k[:, d:]) for k in ks], c)
        a1 = 1.0 / l1
        a2 = lam / l2
        o = functools.reduce(lambda a, b: a + b,
                             [jnp.dot((pa * a1 - pb * a2).astype(BF16), v, preferred_element_type=F32)
                              for pa, pb, v in zip(p1, p2, vs)])
        o = _rms(o, gsub_ref[...]) * (1.0 - lam_init)
    o_ref[...] = (o * _silu(g_ref[...].astype(F32))).astype(o_ref.dtype)


def _attention(q, segs, gate, n_heads, dq, dv, scale, tq, diff=False, lam_params=None, g_sub=None,
               lam_init=0.0):
    b, sq, _ = q.shape
    in_specs = [pl.BlockSpec((None, tq, dq), lambda bi, h, qi: (bi, qi, h))]
    args = [q]
    for k_arr, k_spec, v_arr, v_spec in segs:
        in_specs += [k_spec, v_spec]
        args += [k_arr, v_arr]
    in_specs.append(pl.BlockSpec((None, tq, dv), lambda bi, h, qi: (bi, qi, h)))
    args.append(gate)
    if diff:
        in_specs += [pl.BlockSpec(lam_params.shape, lambda bi, h, qi: (0, 0)),
                     pl.BlockSpec((1, dv), lambda bi, h, qi: (0, 0))]
        args += [lam_params, g_sub.reshape(1, dv)]
    return pl.pallas_call(
        functools.partial(_attn_kernel, n_seg=len(segs), scale=scale, diff=diff, lam_init=lam_init),
        out_shape=jax.ShapeDtypeStruct((b, sq, n_heads * dv), BF16),
        grid=(b, n_heads, sq // tq),
        in_specs=in_specs,
        out_specs=pl.BlockSpec((None, tq, dv), lambda bi, h, qi: (bi, qi, h)),
        compiler_params=_params(("arbitrary", "arbitrary", "arbitrary")),
        name="attn_diff" if diff else "attn",
    )(*args)


def _seg3(arr, sk, width):
    return arr, pl.BlockSpec((None, sk, width), lambda bi, h, qi: (bi, 0, h))


def _seg4(arr, layer, sk, width):
    return arr, pl.BlockSpec((None, None, sk, width), lambda bi, h, qi: (bi, layer, 0, h))


NA_Q_ROWS = 2
NA_K_ROWS = WIN_R + NA_Q_ROWS


def _na_blocks(rows):
    wr = min(WIN_R, rows)
    r = np.arange(rows)
    rs = np.clip(r - wr // 2, 0, rows - wr)
    starts, variants, keys = [], [], {}
    for t in range(rows // NA_Q_ROWS):
        r0 = t * NA_Q_ROWS
        ws = int(np.clip(r0 - wr // 2, 0, rows - NA_K_ROWS))
        sig = (r0 - ws,) + tuple(int(rs[r0 + dq] - ws) for dq in range(NA_Q_ROWS))
        variants.append(keys.setdefault(sig, len(keys)))
        starts.append(ws)
    sigs = sorted(keys, key=keys.get)
    c = np.arange(GRID_W)
    cs = np.clip(c - WIN_C // 2, 0, GRID_W - WIN_C)
    idx = np.zeros((len(sigs), NA_Q_ROWS, GRID_W, NA_K_ROWS, GRID_W), np.int32)
    mask = np.zeros(idx.shape, bool)
    for vi, sig in enumerate(sigs):
        for dq in range(NA_Q_ROWS):
            q_row = sig[0] + dq
            first = sig[1 + dq]
            for j in range(NA_K_ROWS):
                row_ok = first <= j < first + wr
                dr = j - q_row
                dc = c[None, :] - c[:, None]
                col_ok = (c[None, :] >= cs[:, None]) & (c[None, :] < cs[:, None] + WIN_C)
                ok = col_ok & row_ok
                mask[vi, dq, :, j, :] = ok
                idx[vi, dq, :, j, :] = np.where(ok, (dr + WIN_R - 1) * (2 * WIN_C - 1) + dc + WIN_C - 1, 0)
    nq, nk = NA_Q_ROWS * GRID_W, NA_K_ROWS * GRID_W
    return starts, variants, idx.reshape(len(sigs), nq, nk), mask.reshape(len(sigs), nq, nk)


def _na_kernel(q_ref, k_ref, v_ref, kc_ref, vc_ref, bias_ref, g_ref, o_ref, *, scale, starts, variants):
    nq = NA_Q_ROWS * GRID_W
    nk = NA_K_ROWS * GRID_W
    kc = kc_ref[...].astype(BF16)
    vc = vc_ref[...].astype(BF16)

    def body(t, carry):
        start = jnp.int32(starts[0])
        var = jnp.int32(variants[0])
        for ti in range(1, len(starts)):
            start = jnp.where(t == ti, jnp.int32(starts[ti]), start)
            var = jnp.where(t == ti, jnp.int32(variants[ti]), var)
        q0 = pl.multiple_of(t * nq, nq)
        k0 = pl.multiple_of(start * GRID_W, GRID_W)
        q = q_ref[pl.ds(q0, nq), :]
        kw = k_ref[pl.ds(k0, nk), :]
        vw = v_ref[pl.ds(k0, nk), :]
        s_w = _qk(q, kw) * scale + bias_ref[var]
        s_c = _qk(q, kc) * scale
        (p_w, p_c), l = _softmax_parts([s_w, s_c], LOG2E)
        o = (jnp.dot(p_w.astype(BF16), vw, preferred_element_type=F32)
             + jnp.dot(p_c.astype(BF16), vc, preferred_element_type=F32)) / l
        g = g_ref[pl.ds(q0, nq), :].astype(F32)
        o_ref[pl.ds(q0, nq), :] = (o * _silu(g)).astype(o_ref.dtype)
        return carry

    lax.fori_loop(0, len(starts), body, 0)


def _na_latent(q, k, v, cache_k, cache_v, layer, bias, gate, n_heads, d, scale, starts, variants):
    b, s, _ = q.shape
    past = cache_k.shape[2]
    head = pl.BlockSpec((None, s, d), lambda h, bi: (bi, 0, h))
    ctx = pl.BlockSpec((None, None, past, d), lambda h, bi: (bi, layer, 0, h))
    return pl.pallas_call(
        functools.partial(_na_kernel, scale=scale, starts=tuple(starts), variants=tuple(variants)),
        out_shape=jax.ShapeDtypeStruct((b, s, n_heads * d), BF16),
        grid=(n_heads, b),
        in_specs=[head, head, head, ctx, ctx,
                  pl.BlockSpec((None,) + bias.shape[1:], lambda h, bi: (h, 0, 0, 0)),
                  head],
        out_specs=head,
        compiler_params=_params(("arbitrary", "arbitrary")),
        name="na_latent",
    )(q, k, v, cache_k, cache_v, bias, gate)


def _merge_kernel(*refs, n_in):
    o_refs = refs[:n_in]
    w_ref, x_ref, gate_ref, g_ref, out_ref = refs[n_in:]
    y = None
    row = 0
    for o_ref in o_refs:
        width = o_ref.shape[1]
        part = jnp.dot(o_ref[...], w_ref[row:row + width, :], preferred_element_type=F32)
        y = part if y is None else y + part
        row += width
    out_ref[...] = x_ref[...] + gate_ref[...] * _rms(y, g_ref[...])


def _merge(o_list, w_out, x, ada_l, mod_row, g_post, tm=512):
    t, d = x.shape
    in_specs = [pl.BlockSpec((tm, o.shape[1]), lambda i: (i, 0)) for o in o_list]
    in_specs += [
        pl.BlockSpec(w_out.shape, lambda i: (0, 0)),
        pl.BlockSpec((tm, d), lambda i: (i, 0)),
        pl.BlockSpec((None, 1, d), lambda i: (mod_row(i), 0, 2)),
        pl.BlockSpec((1, d), lambda i: (0, 0)),
    ]
    return pl.pallas_call(
        functools.partial(_merge_kernel, n_in=len(o_list)),
        out_shape=jax.ShapeDtypeStruct((t, d), F32),
        grid=(t // tm,),
        in_specs=in_specs,
        out_specs=pl.BlockSpec((tm, d), lambda i: (i, 0)),
        compiler_params=_params(("arbitrary",)),
        name="merge",
    )(*o_list, w_out, x, ada_l, g_post.reshape(1, d))


def _rope_tables(s, n):
    a = n // 2
    half = a // 2
    t = jnp.arange(s)
    inv = ROPE_BASE ** (-jnp.arange(half, dtype=F32) / half)
    ang_r = (t // GRID_W).astype(F32)[:, None] * inv[None, :]
    ang_c = (t % GRID_W).astype(F32)[:, None] * inv[None, :]
    cos = jnp.concatenate([jnp.cos(ang_r)] * 2 + [jnp.cos(ang_c)] * 2, axis=-1)
    sin = jnp.concatenate([jnp.sin(ang_r)] * 2 + [jnp.sin(ang_c)] * 2, axis=-1)
    low = (np.arange(n) % a) < half
    sin_lo = jnp.where(low[None, :], -sin, 0.0)
    sin_hi = jnp.where(low[None, :], 0.0, sin)
    pad = LANES - n
    if pad:
        cos = jnp.pad(cos, ((0, 0), (0, pad)), constant_values=1.0)
        sin_lo = jnp.pad(sin_lo, ((0, 0), (0, pad)))
        sin_hi = jnp.pad(sin_hi, ((0, 0), (0, pad)))
    return (cos, sin_lo, sin_hi), half


def kernel(x_prompt, x_sample, cache_na_k, cache_na_v, cache_mla_ckv, cache_mla_kpe, cache_diff_k, cache_diff_v, c, c_ctx, w_ada, b_ada, g_pre, g_post, w_in_even, w_out_even, na_rpb, mla_g_q, mla_w_uq, mla_g_kv, mla_w_ukv, w_in_odd, w_out_odd, diff_lambda, diff_g):
    bp, sp, d = x_prompt.shape
    bs, ss, _ = x_sample.shape
    depth = w_ada.shape[0]
    past = cache_na_k.shape[2]
    na_heads, na_dim = cache_na_k.shape[3:]
    na_w = na_heads * na_dim
    kv_lora = cache_mla_ckv.shape[3]
    rope_w = cache_mla_kpe.shape[3]
    q_lora = mla_g_q.shape[1]
    diff_heads = cache_diff_k.shape[3]
    diff_w = diff_heads * cache_diff_k.shape[4]
    diff_d = diff_w // (2 * diff_heads)
    mla_w = w_out_even.shape[1] - na_w
    mla_v = mla_nope = LANES
    mla_heads = mla_w // mla_v
    na_scale = na_dim ** -0.5
    mla_scale = (mla_nope + rope_w) ** -0.5
    diff_scale = diff_d ** -0.5
    tp, ts = bp * sp, bs * ss
    tm = 512

    ctx_row = bs
    cond = jnp.concatenate([c, c_ctx[None, :], jnp.zeros((16 - bs - 1, d), F32)], axis=0)
    ada = _ada(cond, w_ada, b_ada)
    row_p = lambda i: ctx_row
    row_s = lambda i: (i * tm) // ss

    rope_diff, half_diff = _rope_tables(ss, diff_d)
    rope_mla, half_mla = _rope_tables(ss, rope_w)

    starts, variants, bias_idx, bias_mask = _na_blocks(ss // GRID_W)

    xp = x_prompt.reshape(tp, d)
    xs = x_sample.reshape(ts, d)
    ck_na = cache_na_k.reshape(bs, -1, past, na_w)
    cv_na = cache_na_v.reshape(bs, -1, past, na_w)
    ck_diff = cache_diff_k.reshape(bs, -1, past, diff_w)
    cv_diff = cache_diff_v.reshape(bs, -1, past, diff_w)
    cache_kpe_pad = jnp.pad(cache_mla_kpe, ((0, 0), (0, 0), (0, 0), (0, LANES - rope_w)))

    new_na_k, new_na_v, new_ckv, new_kpe, new_dk, new_dv = [], [], [], [], [], []
    for l in range(depth):
        i = l // 2
        ada_l = ada[l].reshape(16, 1, 3 * d)
        if l % 2 == 0:
            w_in = w_in_even[i]
            o = 0
            cols = {}
            for name, width in (("qa", na_w), ("ka", na_w), ("va", na_w), ("ga", na_w), ("cq", q_lora),
                                ("ckv", kv_lora), ("kpe", rope_w), ("gb", mla_w)):
                cols[name] = w_in[:, o:o + width]
                o += width
            w_ckvkpe = jnp.concatenate(
                [cols["ckv"], cols["kpe"], jnp.zeros((d, 512 - kv_lora - rope_w), F32)], axis=1)
            weights = [cols["qa"], cols["ka"], cols["va"], cols["ga"], cols["cq"], w_ckvkpe, cols["gb"]]
            weights = [w.astype(BF16) for w in weights]
            w_uq = mla_w_uq[i].reshape(q_lora, mla_heads, mla_nope + rope_w)
            w_uq_cat = jnp.pad(w_uq, ((0, 0), (0, 0), (0, 2 * LANES - mla_nope - rope_w)))
            w_uq_cat = w_uq_cat.reshape(q_lora, mla_heads * 2 * LANES).astype(BF16)
            w_ukv = mla_w_ukv[i].reshape(kv_lora, mla_heads, mla_nope + mla_v)
            w_uk = w_ukv[:, :, :mla_nope].reshape(kv_lora, mla_heads * mla_nope).astype(BF16)
            w_uv = w_ukv[:, :, mla_nope:].reshape(kv_lora, mla_heads * mla_v).astype(BF16)
            w_out = w_out_even[i].astype(BF16)
            rpb_flat = na_rpb[i].reshape(na_heads, -1)
            bias = jnp.where(bias_mask[None], rpb_flat[:, bias_idx], NEG_BIG)

            qa, ka, va, ga, cq, ckvkpe, gb = _proj(
                xp, sp, g_pre[l], ada_l, row_p, weights, [BF16, F32, F32, BF16, F32, F32, BF16], steps=4)
            q_cat = _mla_q(cq, sp, mla_g_q[i], w_uq_cat)
            k_cat, v_mla, ckv_n, kpe_raw = _mla_kv(
                ckvkpe, pl.BlockSpec((tm, kv_lora), lambda i_: (i_, 0)),
                ckvkpe, pl.BlockSpec((tm, LANES), lambda i_: (i_, kv_lora // LANES)),
                tp, sp, w_uk, w_uv, g_kv=mla_g_kv[i], emit_cache=True, kpe_width=rope_w)
            r3 = lambda a: a.reshape(bp, sp, -1)
            oa = _attention(r3(qa), [_seg3(r3(ka), sp, na_dim) + _seg3(r3(va), sp, na_dim)], r3(ga),
                            na_heads, na_dim, na_dim, na_scale, tq=sp)
            ob = _attention(r3(q_cat), [_seg3(r3(k_cat), sp, 2 * LANES) + _seg3(r3(v_mla), sp, mla_v)],
                            r3(gb), mla_heads, 2 * LANES, mla_v, mla_scale, tq=sp)
            xp = _merge([oa.reshape(tp, -1), ob.reshape(tp, -1)], w_out, xp, ada_l, row_p, g_post[l])
            new_na_k.append(r3(ka))
            new_na_v.append(r3(va))
            new_ckv.append(r3(ckv_n))
            new_kpe.append(r3(kpe_raw))

            qa, ka, va, ga, cq, ckvkpe, gb = _proj(
                xs, ss, g_pre[l], ada_l, row_s, weights, [BF16, BF16, BF16, BF16, F32, F32, BF16], steps=4)
            q_cat = _mla_q(cq, ss, mla_g_q[i], w_uq_cat, rope_tables=rope_mla, rope_half=half_mla)
            k_cat, v_mla = _mla_kv(
                ckvkpe, pl.BlockSpec((tm, kv_lora), lambda i_: (i_, 0)),
                ckvkpe, pl.BlockSpec((tm, LANES), lambda i_: (i_, kv_lora // LANES)),
                ts, ss, w_uk, w_uv, g_kv=mla_g_kv[i], rope_tables=rope_mla, rope_half=half_mla)
            tiles_past = past // tm
            kc_cat, vc_mla = _mla_kv(
                cache_mla_ckv, pl.BlockSpec((None, None, tm, kv_lora),
                                            lambda i_: (i_ // tiles_past, i, i_ % tiles_past, 0)),
                cache_kpe_pad, pl.BlockSpec((None, None, tm, LANES),
                                            lambda i_: (i_ // tiles_past, i, i_ % tiles_past, 0)),
                bs * past, past, w_uk, w_uv)
            r3 = lambda a: a.reshape(bs, ss, -1)
            oa = _na_latent(r3(qa), r3(ka), r3(va), ck_na, cv_na, i, bias, r3(ga), na_heads, na_dim,
                            na_scale, starts, variants)
            c3 = lambda a: a.reshape(bs, past, -1)
            ob = _attention(
                r3(q_cat),
                [_seg3(c3(kc_cat), past, 2 * LANES) + _seg3(c3(vc_mla), past, mla_v),
                 _seg3(r3(k_cat), ss, 2 * LANES) + _seg3(r3(v_mla), ss, mla_v)],
                r3(gb), mla_heads, 2 * LANES, mla_v, mla_scale, tq=256)
            xs = _merge([oa.reshape(ts, -1), ob.reshape(ts, -1)], w_out, xs, ada_l, row_s, g_post[l])
        else:
            lam_init = 0.8 - 0.6 * math.exp(-0.3 * l)
            w_in = w_in_odd[i].astype(BF16)
            weights = [w_in[:, k * diff_w:(k + 1) * diff_w] for k in range(4)]
            w_out = w_out_odd[i].astype(BF16)
            dh = 2 * diff_d

            q, k, v, g = _proj(xp, sp, g_pre[l], ada_l, row_p, weights, [BF16, F32, F32, BF16], steps=4)
            r3 = lambda a: a.reshape(bp, sp, -1)
            o = _attention(r3(q), [_seg3(r3(k), sp, dh) + _seg3(r3(v), sp, dh)], r3(g), diff_heads, dh, dh,
                           diff_scale, tq=sp, diff=True, lam_params=diff_lambda[i], g_sub=diff_g[i],
                           lam_init=lam_init)
            xp = _merge([o.reshape(tp, -1)], w_out, xp, ada_l, row_p, g_post[l])
            new_dk.append(r3(k))
            new_dv.append(r3(v))

            q, k, v, g = _proj(xs, ss, g_pre[l], ada_l, row_s, weights, [BF16, BF16, BF16, BF16], steps=4,
                               rope_flags=(True, True, False, False), rope_tables=rope_diff,
                               rope_half=half_diff)
            r3 = lambda a: a.reshape(bs, ss, -1)
            o = _attention(
                r3(q),
                [_seg4(ck_diff, i, past, dh) + _seg4(cv_diff, i, past, dh),
                 _seg3(r3(k), ss, dh) + _seg3(r3(v), ss, dh)],
                r3(g), diff_heads, dh, dh, diff_scale, tq=256, diff=True, lam_params=diff_lambda[i],
                g_sub=diff_g[i], lam_init=lam_init)
            xs = _merge([o.reshape(ts, -1)], w_out, xs, ada_l, row_s, g_post[l])

    stack = lambda parts, tail: jnp.stack(parts, axis=1).reshape((bp, len(parts), sp) + tail)
    return (xp.reshape(bp, sp, d), xs.reshape(bs, ss, d),
            stack(new_na_k, (na_heads, na_dim)), stack(new_na_v, (na_heads, na_dim)),
            stack(new_ckv, (kv_lora,)), stack(new_kpe, (rope_w,)),
            stack(new_dk, (diff_heads, 2 * diff_d)), stack(new_dv, (diff_heads, 2 * diff_d)))
```

```python
import functools
import math

import numpy as np
import jax
import jax.numpy as jnp
from jax import lax
from jax.experimental import pallas as pl
from jax.experimental.pallas import tpu as pltpu

GRID_W = 64
WIN_R = 8
WIN_C = 16
EPS = 1e-6
ROPE_BASE = 10000.0
LOG2E = 1.4426950408889634
NEG_BIG = -1e30

LANES = 128
VMEM_LIMIT = 56 * 1024 * 1024

F32 = jnp.float32
BF16 = jnp.bfloat16


def _params(sem, vmem=VMEM_LIMIT):
    return pltpu.CompilerParams(dimension_semantics=sem, vmem_limit_bytes=vmem)


def _rms(x, g):
    return x * lax.rsqrt(jnp.mean(x * x, axis=-1, keepdims=True) + EPS) * g


def _silu(x):
    return x * (1.0 / (1.0 + jnp.exp(-x)))


def _rope(x, cos, sin_lo, sin_hi, half):
    return (x * cos + pltpu.roll(x, LANES - half, 1) * sin_lo
            + pltpu.roll(x, half, 1) * sin_hi)


def _slot_spec(rows, seq, width, slot, col_of):
    return pl.BlockSpec((rows, None, seq, width),
                        lambda *g: (g[0], slot, 0, col_of(*g)))


def _ada_kernel(cond_ref, w_ref, b_ref, o_ref):
    a = _silu(cond_ref[...]).astype(BF16)
    o_ref[...] = jnp.dot(a, w_ref[...].astype(BF16), preferred_element_type=F32) + b_ref[...]


def _ada(cond, w_ada, b_ada):
    depth, d, n = w_ada.shape
    rows = cond.shape[0]
    tn = 1024
    return pl.pallas_call(
        _ada_kernel,
        out_shape=jax.ShapeDtypeStruct((depth, rows, n), F32),
        grid=(depth, n // tn),
        in_specs=[
            pl.BlockSpec((rows, d), lambda l, j: (0, 0)),
            pl.BlockSpec((None, d, tn), lambda l, j: (l, 0, j)),
            pl.BlockSpec((None, 1, tn), lambda l, j: (l, 0, j)),
        ],
        out_specs=pl.BlockSpec((None, rows, tn), lambda l, j: (l, 0, j)),
        compiler_params=_params(("arbitrary", "arbitrary")),
        name="ada",
    )(cond, w_ada, b_ada.reshape(depth, 1, n))


def _proj_kernel(*refs, n_groups, rope_flags, rope_half, n_alias):
    refs = refs[n_alias:]
    x_ref, g_ref, sh_ref, sc_ref = refs[:4]
    pos = 4
    if any(rope_flags):
        cos_ref, slo_ref, shi_ref = refs[pos:pos + 3]
        pos += 3
    w_refs = refs[pos:pos + n_groups]
    o_refs = refs[pos + n_groups:pos + 2 * n_groups]
    h_ref = refs[pos + 2 * n_groups]

    @pl.when(pl.program_id(1) == 0)
    def _():
        h = _rms(x_ref[...], g_ref[...]) * (1.0 + sc_ref[...]) + sh_ref[...]
        h_ref[...] = h.astype(BF16)

    h = h_ref[...]
    for gi in range(n_groups):
        acc = jnp.dot(h, w_refs[gi][...], preferred_element_type=F32)
        if rope_flags[gi]:
            cos, slo, shi = cos_ref[...], slo_ref[...], shi_ref[...]
            parts = [_rope(acc[:, s:s + LANES], cos, slo, shi, rope_half)
                     for s in range(0, acc.shape[1], LANES)]
            acc = jnp.concatenate(parts, axis=1)
        o_refs[gi][...] = acc.astype(o_refs[gi].dtype).reshape(o_refs[gi].shape)


def _proj(x, seq, g_pre, ada_l, mod_row, weights, out_dtypes, steps, rope_flags=None,
          rope_tables=None, rope_half=0, cache_slots=None, tm=512):
    t, d = x.shape
    n_groups = len(weights)
    rope_flags = tuple(rope_flags) if rope_flags is not None else (False,) * n_groups
    cache_slots = cache_slots or {}
    tiles_per_seq = max(seq // tm, 1)
    aliased = [(gi, cs[2]) for gi, cs in sorted(cache_slots.items()) if cs[2] is not None]
    in_specs = [pl.BlockSpec(memory_space=pl.ANY) for _ in aliased]
    in_specs += [
        pl.BlockSpec((tm, d), lambda i, j: (i, 0)),
        pl.BlockSpec((1, d), lambda i, j: (0, 0)),
        pl.BlockSpec((None, 1, d), lambda i, j: (mod_row(i), 0, 0)),
        pl.BlockSpec((None, 1, d), lambda i, j: (mod_row(i), 0, 1)),
    ]
    args = [prev for _, prev in aliased] + [x, g_pre.reshape(1, d), ada_l, ada_l]
    if any(rope_flags):
        for tab in rope_tables:
            in_specs.append(pl.BlockSpec((tm, LANES), lambda i, j: (i % tiles_per_seq, 0)))
            args.append(tab)
    out_specs, out_shapes = [], []
    for gi, (w, dt) in enumerate(zip(weights, out_dtypes)):
        n = w.shape[1]
        tn = n // steps
        in_specs.append(pl.BlockSpec((d, tn), lambda i, j: (0, j)))
        args.append(w)
        if gi in cache_slots:
            slot, n_slots, _ = cache_slots[gi]
            out_specs.append(_slot_spec(tm // seq, seq, tn, slot, lambda i, j: j))
            out_shapes.append(jax.ShapeDtypeStruct((t // seq, n_slots, seq, n), dt))
        else:
            out_specs.append(pl.BlockSpec((tm, tn), lambda i, j: (i, j)))
            out_shapes.append(jax.ShapeDtypeStruct((t, n), dt))
    return pl.pallas_call(
        functools.partial(_proj_kernel, n_groups=n_groups, rope_flags=rope_flags, rope_half=rope_half,
                          n_alias=len(aliased)),
        out_shape=out_shapes,
        grid=(t // tm, steps),
        in_specs=in_specs,
        out_specs=out_specs,
        input_output_aliases={k: gi for k, (gi, _) in enumerate(aliased)},
        scratch_shapes=[pltpu.VMEM((tm, d), BF16)],
        compiler_params=_params(("arbitrary", "arbitrary")),
        name="proj",
    )(*args)


def _mla_q_kernel(*refs, rope, rope_half):
    if rope:
        cq_ref, g_ref, w_ref, cos_ref, slo_ref, shi_ref, o_ref = refs
    else:
        cq_ref, g_ref, w_ref, o_ref = refs
    n = _rms(cq_ref[...], g_ref[...]).astype(BF16)
    q = jnp.dot(n, w_ref[...], preferred_element_type=F32)
    if rope:
        cos, slo, shi = cos_ref[...], slo_ref[...], shi_ref[...]
        parts = []
        for s in range(0, q.shape[1], 2 * LANES):
            parts.append(q[:, s:s + LANES])
            parts.append(_rope(q[:, s + LANES:s + 2 * LANES], cos, slo, shi, rope_half))
        q = jnp.concatenate(parts, axis=1)
    o_ref[...] = q.astype(o_ref.dtype)


def _mla_q(cq, seq, g_q, w_uq_cat, rope_tables=None, rope_half=0, tm=512):
    t, r = cq.shape
    n = w_uq_cat.shape[1]
    rope = rope_tables is not None
    tiles_per_seq = max(seq // tm, 1)
    in_specs = [
        pl.BlockSpec((tm, r), lambda i: (i, 0)),
        pl.BlockSpec((1, r), lambda i: (0, 0)),
        pl.BlockSpec((r, n), lambda i: (0, 0)),
    ]
    args = [cq, g_q.reshape(1, r), w_uq_cat]
    if rope:
        for tab in rope_tables:
            in_specs.append(pl.BlockSpec((tm, LANES), lambda i: (i % tiles_per_seq, 0)))
            args.append(tab)
    return pl.pallas_call(
        functools.partial(_mla_q_kernel, rope=rope, rope_half=rope_half),
        out_shape=jax.ShapeDtypeStruct((t, n), BF16),
        grid=(t // tm,),
        in_specs=in_specs,
        out_specs=pl.BlockSpec((tm, n), lambda i: (i, 0)),
        compiler_params=_params(("arbitrary",)),
        name="mla_q",
    )(*args)


def _mla_kv_kernel(*refs, normalise, rope, rope_half, emit_cache, n_alias, n_heads, nope):
    refs = list(refs[n_alias:])
    ckv_ref, kpe_ref = refs[:2]
    pos = 2
    if normalise:
        g_ref = refs[pos]
        pos += 1
    wk_ref, wv_ref = refs[pos:pos + 2]
    pos += 2
    if rope:
        cos_ref, slo_ref, shi_ref = refs[pos:pos + 3]
        pos += 3
    kcat_ref, v_ref = refs[pos:pos + 2]
    pos += 2
    ckv = ckv_ref[...]
    if normalise:
        ckv = _rms(ckv, g_ref[...])
    kpe = kpe_ref[...]
    if emit_cache:
        ckv_out_ref, kpe_out_ref = refs[pos:pos + 2]
        ckv_out_ref[...] = ckv.reshape(ckv_out_ref.shape)
        kpe_out_ref[...] = kpe[:, :kpe_out_ref.shape[-1]].reshape(kpe_out_ref.shape)
    if rope:
        kpe = _rope(kpe, cos_ref[...], slo_ref[...], shi_ref[...], rope_half)
    cb = ckv.astype(BF16)
    kn = jnp.dot(cb, wk_ref[...], preferred_element_type=F32).astype(BF16)
    v_ref[...] = jnp.dot(cb, wv_ref[...], preferred_element_type=F32).astype(BF16)
    kpe_b = kpe.astype(BF16)
    for h in range(n_heads):
        kcat_ref[:, h * 2 * nope:h * 2 * nope + nope] = kn[:, h * nope:(h + 1) * nope]
        kcat_ref[:, h * 2 * nope + nope:(h + 1) * 2 * nope] = kpe_b


def _mla_kv(ckv_arr, ckv_spec, kpe_arr, kpe_spec, t, seq, w_uk, w_uv, g_kv=None, rope_tables=None,
            rope_half=0, cache=None, kpe_width=64, tm=512):
    lora, nk = w_uk.shape
    nv = w_uv.shape[1]
    nope = LANES
    n_heads = nk // nope
    normalise = g_kv is not None
    rope = rope_tables is not None
    tiles_per_seq = max(seq // tm, 1)
    aliased = [] if cache is None or cache[2] is None else [cache[2], cache[3]]
    in_specs = [pl.BlockSpec(memory_space=pl.ANY) for _ in aliased] + [ckv_spec, kpe_spec]
    args = aliased + [ckv_arr, kpe_arr]
    if normalise:
        in_specs.append(pl.BlockSpec((1, lora), lambda i: (0, 0)))
        args.append(g_kv.reshape(1, lora))
    in_specs += [pl.BlockSpec((lora, nk), lambda i: (0, 0)), pl.BlockSpec((lora, nv), lambda i: (0, 0))]
    args += [w_uk, w_uv]
    if rope:
        for tab in rope_tables:
            in_specs.append(pl.BlockSpec((tm, LANES), lambda i: (i % tiles_per_seq, 0)))
            args.append(tab)
    out_shapes = [jax.ShapeDtypeStruct((t, 2 * nk), BF16), jax.ShapeDtypeStruct((t, nv), BF16)]
    out_specs = [pl.BlockSpec((tm, 2 * nk), lambda i: (i, 0)), pl.BlockSpec((tm, nv), lambda i: (i, 0))]
    if cache is not None:
        slot, n_slots = cache[:2]
        out_shapes += [jax.ShapeDtypeStruct((t // seq, n_slots, seq, lora), F32),
                       jax.ShapeDtypeStruct((t // seq, n_slots, seq, kpe_width), F32)]
        out_specs += [_slot_spec(tm // seq, seq, lora, slot, lambda i: 0),
                      _slot_spec(tm // seq, seq, kpe_width, slot, lambda i: 0)]
    return pl.pallas_call(
        functools.partial(_mla_kv_kernel, normalise=normalise, rope=rope, rope_half=rope_half,
                          emit_cache=cache is not None, n_alias=len(aliased), n_heads=n_heads, nope=nope),
        out_shape=out_shapes,
        grid=(t // tm,),
        in_specs=in_specs,
        out_specs=out_specs,
        input_output_aliases={k: 2 + k for k in range(len(aliased))},
        compiler_params=_params(("arbitrary",)),
        name="mla_kv",
    )(*args)


def _softmax_parts(s_list, c):
    m = functools.reduce(jnp.maximum, [jnp.max(s, axis=-1, keepdims=True) for s in s_list])
    p_list = [jnp.exp2((s - m) * c) for s in s_list]
    l = functools.reduce(lambda a, b: a + b, [jnp.sum(p, axis=-1, keepdims=True) for p in p_list])
    return p_list, l


def _qk(q, k):
    return lax.dot_general(q, k, (((1,), (1,)), ((), ())), preferred_element_type=F32)


def _attn_kernel(*refs, n_seg, scale, diff, lam_init, hps, dq, dv):
    q_ref = refs[0]
    kv_refs = refs[1:1 + 2 * n_seg]
    g_ref = refs[1 + 2 * n_seg]
    pos = 2 + 2 * n_seg
    if diff:
        lam_ref, gsub_ref = refs[pos:pos + 2]
        pos += 2
        lp = lam_ref[...]
        lam = (jnp.exp(jnp.sum(lp[0:1] * lp[1:2], axis=-1, keepdims=True))
               - jnp.exp(jnp.sum(lp[2:3] * lp[3:4], axis=-1, keepdims=True)) + lam_init)
    o_ref = refs[pos]
    c = scale * LOG2E
    for hh in range(hps):
        q = q_ref[:, hh * dq:(hh + 1) * dq].astype(BF16)
        ks = [kv_refs[2 * i][:, hh * dq:(hh + 1) * dq].astype(BF16) for i in range(n_seg)]
        vs = [kv_refs[2 * i + 1][:, hh * dv:(hh + 1) * dv].astype(BF16) for i in range(n_seg)]
        if not diff:
            p_list, l = _softmax_parts([_qk(q, k) for k in ks], c)
            o = functools.reduce(lambda a, b: a + b,
                                 [jnp.dot(p.astype(BF16), v, preferred_element_type=F32)
                                  for p, v in zip(p_list, vs)])
            o = o / l
        else:
            d = dq // 2
            p1, l1 = _softmax_parts([_qk(q[:, :d], k[:, :d]) for k in ks], c)
            p2, l2 = _softmax_parts([_qk(q[:, d:], k[:, d:]) for k in ks], c)
            a1 = 1.0 / l1
            a2 = lam / l2
            o = functools.reduce(lambda a, b: a + b,
                                 [jnp.dot((pa * a1 - pb * a2).astype(BF16), v, preferred_element_type=F32)
                                  for pa, pb, v in zip(p1, p2, vs)])
            o = _rms(o, gsub_ref[...]) * (1.0 - lam_init)
        g = g_ref[:, hh * dv:(hh + 1) * dv].astype(F32)
        o_ref[:, hh * dv:(hh + 1) * dv] = (o * _silu(g)).astype(o_ref.dtype)


def _attention(q, segs, gate, n_heads, dq, dv, scale, tq, hps=1, diff=False, lam_params=None,
               g_sub=None, lam_init=0.0):
    b, sq, _ = q.shape
    in_specs = [pl.BlockSpec((None, tq, hps * dq), lambda bi, h, qi: (bi, qi, h))]
    args = [q]
    for k_arr, k_spec, v_arr, v_spec in segs:
        in_specs += [k_spec, v_spec]
        args += [k_arr, v_arr]
    in_specs.append(pl.BlockSpec((None, tq, hps * dv), lambda bi, h, qi: (bi, qi, h)))
    args.append(gate)
    if diff:
        in_specs += [pl.BlockSpec(lam_params.shape, lambda bi, h, qi: (0, 0)),
                     pl.BlockSpec((1, dv), lambda bi, h, qi: (0, 0))]
        args += [lam_params, g_sub.reshape(1, dv)]
    return pl.pallas_call(
        functools.partial(_attn_kernel, n_seg=len(segs), scale=scale, diff=diff, lam_init=lam_init,
                          hps=hps, dq=dq, dv=dv),
        out_shape=jax.ShapeDtypeStruct((b, sq, n_heads * dv), BF16),
        grid=(b, n_heads // hps, sq // tq),
        in_specs=in_specs,
        out_specs=pl.BlockSpec((None, tq, hps * dv), lambda bi, h, qi: (bi, qi, h)),
        compiler_params=_params(("arbitrary", "arbitrary", "arbitrary")),
        name="attn_diff" if diff else "attn",
    )(*args)


def _seg3(arr, sk, width):
    return arr, pl.BlockSpec((None, sk, width), lambda bi, h, qi: (bi, 0, h))


def _seg4(arr, layer, sk, width):
    return arr, pl.BlockSpec((None, None, sk, width), lambda bi, h, qi: (bi, layer, 0, h))


NA_Q_ROWS = 2
NA_K_ROWS = WIN_R + NA_Q_ROWS
NA_K_PAIRS = NA_K_ROWS * GRID_W // LANES
RPB_W = 2 * WIN_C - 1


def _na_blocks(rows):
    assert 2 * GRID_W == LANES and rows >= NA_K_ROWS
    wr = min(WIN_R, rows)
    r = np.arange(rows)
    rs = np.clip(r - wr // 2, 0, rows - wr)
    starts, variants, keys = [], [], {}
    for t in range(rows // NA_Q_ROWS):
        r0 = t * NA_Q_ROWS
        ws = int(np.clip(r0 - wr // 2, 0, rows - NA_K_ROWS))
        sig = (r0 - ws,) + tuple(int(rs[r0 + dq] - ws) for dq in range(NA_Q_ROWS))
        assert all(0 <= f and f + wr <= NA_K_ROWS for f in sig[1:])
        variants.append(keys.setdefault(sig, len(keys)))
        starts.append(ws)
    sigs = sorted(keys, key=keys.get)
    idx = np.zeros((len(sigs), NA_Q_ROWS, NA_K_PAIRS, LANES), np.int32)
    kind = np.ones(idx.shape, np.int32)
    for vi, sig in enumerate(sigs):
        for dq in range(NA_Q_ROWS):
            for j in range(NA_K_ROWS):
                lanes = slice((j % 2) * GRID_W, (j % 2) * GRID_W + RPB_W)
                if sig[1 + dq] <= j < sig[1 + dq] + wr:
                    dr = j - (sig[0] + dq)
                    idx[vi, dq, j // 2, lanes] = (dr + WIN_R - 1) * RPB_W + np.arange(RPB_W)
                    kind[vi, dq, j // 2, lanes] = 0
                else:
                    kind[vi, dq, j // 2, (j % 2) * GRID_W:(j % 2 + 1) * GRID_W] = 2
    return starts, variants, idx, kind


def _na_kernel(q_ref, k_ref, v_ref, kc_ref, vc_ref, rows_ref, g_ref, o_ref, bias_ref, *, scale, starts,
               variants):
    nq = NA_Q_ROWS * GRID_W
    nk = NA_K_ROWS * GRID_W

    @pl.when(pl.program_id(1) == 0)
    def _():
        ci = lax.broadcasted_iota(jnp.int32, (GRID_W, LANES), 0)
        kc = lax.broadcasted_iota(jnp.int32, (GRID_W, LANES), 1) & (GRID_W - 1)
        cs = jnp.clip(ci - WIN_C // 2, 0, GRID_W - WIN_C)
        col_mask = jnp.where((kc >= cs) & (kc < cs + WIN_C), 0.0, NEG_BIG)
        for vi in range(bias_ref.shape[0]):
            for dq in range(NA_Q_ROWS):
                vecs = rows_ref[vi, dq]
                for jp in range(NA_K_PAIRS):
                    w = jnp.broadcast_to(vecs[jp:jp + 1, :], (GRID_W, LANES))
                    tile = pltpu.roll(w, LANES - (WIN_C - 1), 1, stride=1, stride_axis=0)
                    bias_ref[vi, dq * GRID_W:(dq + 1) * GRID_W, jp * LANES:(jp + 1) * LANES] = tile + col_mask

    kc = kc_ref[...].astype(BF16)
    vc = vc_ref[...].astype(BF16)

    def body(t, carry):
        start = jnp.int32(starts[0])
        var = jnp.int32(variants[0])
        for ti in range(1, len(starts)):
            start = jnp.where(t == ti, jnp.int32(starts[ti]), start)
            var = jnp.where(t == ti, jnp.int32(variants[ti]), var)
        q0 = pl.multiple_of(t * nq, nq)
        k0 = pl.multiple_of(start * GRID_W, GRID_W)
        q = q_ref[pl.ds(q0, nq), :]
        kw = k_ref[pl.ds(k0, nk), :]
        vw = v_ref[pl.ds(k0, nk), :]
        s_w = _qk(q, kw) * scale + bias_ref[var]
        s_c = _qk(q, kc) * scale
        (p_w, p_c), l = _softmax_parts([s_w, s_c], LOG2E)
        o = (jnp.dot(p_w.astype(BF16), vw, preferred_element_type=F32)
             + jnp.dot(p_c.astype(BF16), vc, preferred_element_type=F32)) / l
        g = g_ref[pl.ds(q0, nq), :].astype(F32)
        o_ref[pl.ds(q0, nq), :] = (o * _silu(g)).astype(o_ref.dtype)
        return carry

    lax.fori_loop(0, len(starts), body, 0)


def _na_latent(q, k, v, cache_k, cache_v, layer, bias_rows, gate, n_heads, d, scale, starts, variants):
    b, s, _ = q.shape
    past = cache_k.shape[2]
    n_var = bias_rows.shape[1]
    head = pl.BlockSpec((None, s, d), lambda h, bi: (bi, 0, h))
    ctx = pl.BlockSpec((None, None, past, d), lambda h, bi: (bi, layer, 0, h))
    return pl.pallas_call(
        functools.partial(_na_kernel, scale=scale, starts=tuple(starts), variants=tuple(variants)),
        out_shape=jax.ShapeDtypeStruct((b, s, n_heads * d), BF16),
        grid=(n_heads, b),
        in_specs=[head, head, head, ctx, ctx,
                  pl.BlockSpec((None,) + bias_rows.shape[1:], lambda h, bi: (h, 0, 0, 0, 0)),
                  head],
        out_specs=head,
        scratch_shapes=[pltpu.VMEM((n_var, NA_Q_ROWS * GRID_W, NA_K_ROWS * GRID_W), F32)],
        compiler_params=_params(("arbitrary", "arbitrary")),
        name="na_latent",
    )(q, k, v, cache_k, cache_v, bias_rows, gate)


def _merge_kernel(*refs, n_in):
    o_refs = refs[:n_in]
    w_ref, x_ref, gate_ref, g_ref, out_ref = refs[n_in:]
    y = None
    row = 0
    for o_ref in o_refs:
        width = o_ref.shape[1]
        part = jnp.dot(o_ref[...], w_ref[row:row + width, :], preferred_element_type=F32)
        y = part if y is None else y + part
        row += width
    out_ref[...] = x_ref[...] + gate_ref[...] * _rms(y, g_ref[...])


def _merge(o_list, w_out, x, ada_l, mod_row, g_post, tm=512):
    t, d = x.shape
    in_specs = [pl.BlockSpec((tm, o.shape[1]), lambda i: (i, 0)) for o in o_list]
    in_specs += [
        pl.BlockSpec(w_out.shape, lambda i: (0, 0)),
        pl.BlockSpec((tm, d), lambda i: (i, 0)),
        pl.BlockSpec((None, 1, d), lambda i: (mod_row(i), 0, 2)),
        pl.BlockSpec((1, d), lambda i: (0, 0)),
    ]
    return pl.pallas_call(
        functools.partial(_merge_kernel, n_in=len(o_list)),
        out_shape=jax.ShapeDtypeStruct((t, d), F32),
        grid=(t // tm,),
        in_specs=in_specs,
        out_specs=pl.BlockSpec((tm, d), lambda i: (i, 0)),
        compiler_params=_params(("arbitrary",)),
        name="merge",
    )(*o_list, w_out, x, ada_l, g_post.reshape(1, d))


def _rope_tables(s, n):
    a = n // 2
    half = a // 2
    t = jnp.arange(s)
    inv = ROPE_BASE ** (-jnp.arange(half, dtype=F32) / half)
    ang_r = (t // GRID_W).astype(F32)[:, None] * inv[None, :]
    ang_c = (t % GRID_W).astype(F32)[:, None] * inv[None, :]
    cos = jnp.concatenate([jnp.cos(ang_r)] * 2 + [jnp.cos(ang_c)] * 2, axis=-1)
    sin = jnp.concatenate([jnp.sin(ang_r)] * 2 + [jnp.sin(ang_c)] * 2, axis=-1)
    low = (np.arange(n) % a) < half
    sin_lo = jnp.where(low[None, :], -sin, 0.0)
    sin_hi = jnp.where(low[None, :], 0.0, sin)
    pad = LANES - n
    if pad:
        cos = jnp.pad(cos, ((0, 0), (0, pad)), constant_values=1.0)
        sin_lo = jnp.pad(sin_lo, ((0, 0), (0, pad)))
        sin_hi = jnp.pad(sin_hi, ((0, 0), (0, pad)))
    return (cos, sin_lo, sin_hi), half


def kernel(x_prompt, x_sample, cache_na_k, cache_na_v, cache_mla_ckv, cache_mla_kpe, cache_diff_k, cache_diff_v, c, c_ctx, w_ada, b_ada, g_pre, g_post, w_in_even, w_out_even, na_rpb, mla_g_q, mla_w_uq, mla_g_kv, mla_w_ukv, w_in_odd, w_out_odd, diff_lambda, diff_g):
    bp, sp, d = x_prompt.shape
    bs, ss, _ = x_sample.shape
    depth = w_ada.shape[0]
    n_even, n_odd = w_in_even.shape[0], w_in_odd.shape[0]
    past = cache_na_k.shape[2]
    na_heads, na_dim = cache_na_k.shape[3:]
    na_w = na_heads * na_dim
    kv_lora = cache_mla_ckv.shape[3]
    rope_w = cache_mla_kpe.shape[3]
    q_lora = mla_g_q.shape[1]
    diff_heads = cache_diff_k.shape[3]
    diff_w = diff_heads * cache_diff_k.shape[4]
    diff_d = diff_w // (2 * diff_heads)
    mla_w = w_out_even.shape[1] - na_w
    mla_v = mla_nope = LANES
    mla_heads = mla_w // mla_v
    na_scale = na_dim ** -0.5
    mla_scale = (mla_nope + rope_w) ** -0.5
    diff_scale = diff_d ** -0.5
    tp, ts = bp * sp, bs * ss
    tm = 512

    ctx_row = bs
    cond = jnp.concatenate([c, c_ctx[None, :], jnp.zeros((16 - bs - 1, d), F32)], axis=0)
    ada = _ada(cond, w_ada, b_ada)
    row_p = lambda i: ctx_row
    row_s = lambda i: (i * tm) // ss

    rope_diff, half_diff = _rope_tables(ss, diff_d)
    rope_mla, half_mla = _rope_tables(ss, rope_w)

    starts, variants, rows_idx, rows_kind = _na_blocks(ss // GRID_W)

    xp = x_prompt.reshape(tp, d)
    xs = x_sample.reshape(ts, d)
    ck_na = cache_na_k.reshape(bs, -1, past, na_w)
    cv_na = cache_na_v.reshape(bs, -1, past, na_w)
    ck_diff = cache_diff_k.reshape(bs, -1, past, diff_w)
    cv_diff = cache_diff_v.reshape(bs, -1, past, diff_w)
    cache_kpe_pad = jnp.pad(cache_mla_kpe, ((0, 0), (0, 0), (0, 0), (0, LANES - rope_w)))

    new_na_k = new_na_v = new_ckv = new_kpe = new_dk = new_dv = None
    for l in range(depth):
        i = l // 2
        ada_l = ada[l].reshape(16, 1, 3 * d)
        if l % 2 == 0:
            w_in = w_in_even[i]
            o = 0
            cols = {}
            for name, width in (("qa", na_w), ("ka", na_w), ("va", na_w), ("ga", na_w), ("cq", q_lora),
                                ("ckv", kv_lora), ("kpe", rope_w), ("gb", mla_w)):
                cols[name] = w_in[:, o:o + width]
                o += width
            w_ckvkpe = jnp.concatenate(
                [cols["ckv"], cols["kpe"], jnp.zeros((d, 512 - kv_lora - rope_w), F32)], axis=1)
            weights = [cols["qa"], cols["ka"], cols["va"], cols["ga"], cols["cq"], w_ckvkpe, cols["gb"]]
            weights = [w.astype(BF16) for w in weights]
            w_uq = mla_w_uq[i].reshape(q_lora, mla_heads, mla_nope + rope_w)
            w_uq_cat = jnp.pad(w_uq, ((0, 0), (0, 0), (0, 2 * LANES - mla_nope - rope_w)))
            w_uq_cat = w_uq_cat.reshape(q_lora, mla_heads * 2 * LANES).astype(BF16)
            w_ukv = mla_w_ukv[i].reshape(kv_lora, mla_heads, mla_nope + mla_v)
            w_uk = w_ukv[:, :, :mla_nope].reshape(kv_lora, mla_heads * mla_nope).astype(BF16)
            w_uv = w_ukv[:, :, mla_nope:].reshape(kv_lora, mla_heads * mla_v).astype(BF16)
            w_out = w_out_even[i].astype(BF16)
            rpb_flat = na_rpb[i].reshape(na_heads, -1)
            bias_rows = jnp.where(rows_kind[None] == 0, rpb_flat[:, rows_idx],
                                  jnp.where(rows_kind[None] == 1, 0.0, NEG_BIG))

            qa, new_na_k, new_na_v, ga, cq, ckvkpe, gb = _proj(
                xp, sp, g_pre[l], ada_l, row_p, weights, [BF16, F32, F32, BF16, F32, F32, BF16], steps=4,
                cache_slots={1: (i, n_even, new_na_k), 2: (i, n_even, new_na_v)})
            q_cat = _mla_q(cq, sp, mla_g_q[i], w_uq_cat)
            k_cat, v_mla, new_ckv, new_kpe = _mla_kv(
                ckvkpe, pl.BlockSpec((tm, kv_lora), lambda i_: (i_, 0)),
                ckvkpe, pl.BlockSpec((tm, LANES), lambda i_: (i_, kv_lora // LANES)),
                tp, sp, w_uk, w_uv, g_kv=mla_g_kv[i], cache=(i, n_even, new_ckv, new_kpe), kpe_width=rope_w)
            r3 = lambda a: a.reshape(bp, sp, -1)
            oa = _attention(r3(qa), [_seg4(new_na_k, i, sp, na_w) + _seg4(new_na_v, i, sp, na_w)], r3(ga),
                            na_heads, na_dim, na_dim, na_scale, tq=sp, hps=na_heads)
            ob = _attention(r3(q_cat), [_seg3(r3(k_cat), sp, mla_heads * 2 * LANES)
                                        + _seg3(r3(v_mla), sp, mla_w)],
                            r3(gb), mla_heads, 2 * LANES, mla_v, mla_scale, tq=sp, hps=mla_heads)
            xp = _merge([oa.reshape(tp, -1), ob.reshape(tp, -1)], w_out, xp, ada_l, row_p, g_post[l])

            qa, ka, va, ga, cq, ckvkpe, gb = _proj(
                xs, ss, g_pre[l], ada_l, row_s, weights, [BF16, BF16, BF16, BF16, F32, F32, BF16], steps=4)
            q_cat = _mla_q(cq, ss, mla_g_q[i], w_uq_cat, rope_tables=rope_mla, rope_half=half_mla)
            k_cat, v_mla = _mla_kv(
                ckvkpe, pl.BlockSpec((tm, kv_lora), lambda i_: (i_, 0)),
                ckvkpe, pl.BlockSpec((tm, LANES), lambda i_: (i_, kv_lora // LANES)),
                ts, ss, w_uk, w_uv, g_kv=mla_g_kv[i], rope_tables=rope_mla, rope_half=half_mla)
            tiles_past = past // tm
            kc_cat, vc_mla = _mla_kv(
                cache_mla_ckv, pl.BlockSpec((None, None, tm, kv_lora),
                                            lambda i_: (i_ // tiles_past, i, i_ % tiles_past, 0)),
                cache_kpe_pad, pl.BlockSpec((None, None, tm, LANES),
                                            lambda i_: (i_ // tiles_past, i, i_ % tiles_past, 0)),
                bs * past, past, w_uk, w_uv)
            r3 = lambda a: a.reshape(bs, ss, -1)
            oa = _na_latent(r3(qa), r3(ka), r3(va), ck_na, cv_na, i, bias_rows, r3(ga), na_heads, na_dim,
                            na_scale, starts, variants)
            c3 = lambda a: a.reshape(bs, past, -1)
            ob = _attention(
                r3(q_cat),
                [_seg3(c3(kc_cat), past, 2 * LANES) + _seg3(c3(vc_mla), past, mla_v),
                 _seg3(r3(k_cat), ss, 2 * LANES) + _seg3(r3(v_mla), ss, mla_v)],
                r3(gb), mla_heads, 2 * LANES, mla_v, mla_scale, tq=256)
            xs = _merge([oa.reshape(ts, -1), ob.reshape(ts, -1)], w_out, xs, ada_l, row_s, g_post[l])
        else:
            lam_init = 0.8 - 0.6 * math.exp(-0.3 * l)
            w_in = w_in_odd[i].astype(BF16)
            weights = [w_in[:, k * diff_w:(k + 1) * diff_w] for k in range(4)]
            w_out = w_out_odd[i].astype(BF16)
            dh = 2 * diff_d

            q, new_dk, new_dv, g = _proj(
                xp, sp, g_pre[l], ada_l, row_p, weights, [BF16, F32, F32, BF16], steps=4,
                cache_slots={1: (i, n_odd, new_dk), 2: (i, n_odd, new_dv)})
            r3 = lambda a: a.reshape(bp, sp, -1)
            o = _attention(r3(q), [_seg4(new_dk, i, sp, diff_w) + _seg4(new_dv, i, sp, diff_w)], r3(g),
                           diff_heads, dh, dh, diff_scale, tq=sp, hps=diff_heads, diff=True,
                           lam_params=diff_lambda[i], g_sub=diff_g[i], lam_init=lam_init)
            xp = _merge([o.reshape(tp, -1)], w_out, xp, ada_l, row_p, g_post[l])

            q, k, v, g = _proj(xs, ss, g_pre[l], ada_l, row_s, weights, [BF16, BF16, BF16, BF16], steps=4,
                               rope_flags=(True, True, False, False), rope_tables=rope_diff,
                               rope_half=half_diff)
            r3 = lambda a: a.reshape(bs, ss, -1)
            o = _attention(
                r3(q),
                [_seg4(ck_diff, i, past, dh) + _seg4(cv_diff, i, past, dh),
                 _seg3(r3(k), ss, dh) + _seg3(r3(v), ss, dh)],
                r3(g), diff_heads, dh, dh, diff_scale, tq=256, diff=True, lam_params=diff_lambda[i],
                g_sub=diff_g[i], lam_init=lam_init)
            xs = _merge([o.reshape(ts, -1)], w_out, xs, ada_l, row_s, g_post[l])

    return (xp.reshape(bp, sp, d), xs.reshape(bs, ss, d),
            new_na_k.reshape(bp, n_even, sp, na_heads, na_dim),
            new_na_v.reshape(bp, n_even, sp, na_heads, na_dim),
            new_ckv, new_kpe,
            new_dk.reshape(bp, n_odd, sp, diff_heads, 2 * diff_d),
            new_dv.reshape(bp, n_odd, sp, diff_heads, 2 * diff_d))
```

```python
import functools
import math

import numpy as np
import jax
import jax.numpy as jnp
from jax import lax
from jax.experimental import pallas as pl
from jax.experimental.pallas import tpu as pltpu

GRID_W = 64
WIN_R = 8
WIN_C = 16
EPS = 1e-6
ROPE_BASE = 10000.0
LOG2E = 1.4426950408889634
NEG_BIG = -1e30

LANES = 128
VMEM_LIMIT = 56 * 1024 * 1024

F32 = jnp.float32
BF16 = jnp.bfloat16


def _params(sem, vmem=VMEM_LIMIT):
    return pltpu.CompilerParams(dimension_semantics=sem, vmem_limit_bytes=vmem)


def _rms(x, g):
    return x * lax.rsqrt(jnp.mean(x * x, axis=-1, keepdims=True) + EPS) * g


def _silu(x):
    return x * (1.0 / (1.0 + jnp.exp(-x)))


def _rope(x, cos, sin_lo, sin_hi, half):
    return (x * cos + pltpu.roll(x, LANES - half, 1) * sin_lo
            + pltpu.roll(x, half, 1) * sin_hi)


def _slot_spec(rows, seq, width, slot, col_of):
    return pl.BlockSpec((rows, None, seq, width),
                        lambda *g: (g[0], slot, 0, col_of(*g)))


def _ada_kernel(cond_ref, w_ref, b_ref, o_ref):
    a = _silu(cond_ref[...]).astype(BF16)
    o_ref[...] = jnp.dot(a, w_ref[...].astype(BF16), preferred_element_type=F32) + b_ref[...]


def _ada(cond, w_ada, b_ada):
    depth, d, n = w_ada.shape
    rows = cond.shape[0]
    tn = 1024
    return pl.pallas_call(
        _ada_kernel,
        out_shape=jax.ShapeDtypeStruct((depth, rows, n), F32),
        grid=(depth, n // tn),
        in_specs=[
            pl.BlockSpec((rows, d), lambda l, j: (0, 0)),
            pl.BlockSpec((None, d, tn), lambda l, j: (l, 0, j)),
            pl.BlockSpec((None, 1, tn), lambda l, j: (l, 0, j)),
        ],
        out_specs=pl.BlockSpec((None, rows, tn), lambda l, j: (l, 0, j)),
        compiler_params=_params(("arbitrary", "arbitrary")),
        name="ada",
    )(cond, w_ada, b_ada.reshape(depth, 1, n))


def _proj_kernel(*refs, n_groups, rope_flags, rope_half, out_scales, n_alias):
    refs = refs[n_alias:]
    x_ref, g_ref, sh_ref, sc_ref = refs[:4]
    pos = 4
    if any(rope_flags):
        cos_ref, slo_ref, shi_ref = refs[pos:pos + 3]
        pos += 3
    w_refs = refs[pos:pos + n_groups]
    o_refs = refs[pos + n_groups:pos + 2 * n_groups]
    h_ref = refs[pos + 2 * n_groups]

    @pl.when(pl.program_id(1) == 0)
    def _():
        h = _rms(x_ref[...], g_ref[...]) * (1.0 + sc_ref[...]) + sh_ref[...]
        h_ref[...] = h.astype(BF16)

    h = h_ref[...]
    for gi in range(n_groups):
        acc = jnp.dot(h, w_refs[gi][...], preferred_element_type=F32)
        if rope_flags[gi]:
            cos, slo, shi = cos_ref[...], slo_ref[...], shi_ref[...]
            parts = [_rope(acc[:, s:s + LANES], cos, slo, shi, rope_half)
                     for s in range(0, acc.shape[1], LANES)]
            acc = jnp.concatenate(parts, axis=1)
        if out_scales[gi] != 1.0:
            acc = acc * out_scales[gi]
        o_refs[gi][...] = acc.astype(o_refs[gi].dtype).reshape(o_refs[gi].shape)


def _proj(x, seq, g_pre, ada_l, mod_row, weights, out_dtypes, steps, rope_flags=None,
          rope_tables=None, rope_half=0, out_scales=None, cache_slots=None, tm=512):
    t, d = x.shape
    n_groups = len(weights)
    rope_flags = tuple(rope_flags) if rope_flags is not None else (False,) * n_groups
    out_scales = tuple(out_scales) if out_scales is not None else (1.0,) * n_groups
    cache_slots = cache_slots or {}
    tiles_per_seq = max(seq // tm, 1)
    aliased = [(gi, cs[2]) for gi, cs in sorted(cache_slots.items()) if cs[2] is not None]
    in_specs = [pl.BlockSpec(memory_space=pl.ANY) for _ in aliased]
    in_specs += [
        pl.BlockSpec((tm, d), lambda i, j: (i, 0)),
        pl.BlockSpec((1, d), lambda i, j: (0, 0)),
        pl.BlockSpec((None, 1, d), lambda i, j: (mod_row(i), 0, 0)),
        pl.BlockSpec((None, 1, d), lambda i, j: (mod_row(i), 0, 1)),
    ]
    args = [prev for _, prev in aliased] + [x, g_pre.reshape(1, d), ada_l, ada_l]
    if any(rope_flags):
        for tab in rope_tables:
            in_specs.append(pl.BlockSpec((tm, LANES), lambda i, j: (i % tiles_per_seq, 0)))
            args.append(tab)
    out_specs, out_shapes = [], []
    for gi, (w, dt) in enumerate(zip(weights, out_dtypes)):
        n = w.shape[1]
        tn = n // steps
        in_specs.append(pl.BlockSpec((d, tn), lambda i, j: (0, j)))
        args.append(w)
        if gi in cache_slots:
            slot, n_slots, _ = cache_slots[gi]
            out_specs.append(_slot_spec(tm // seq, seq, tn, slot, lambda i, j: j))
            out_shapes.append(jax.ShapeDtypeStruct((t // seq, n_slots, seq, n), dt))
        else:
            out_specs.append(pl.BlockSpec((tm, tn), lambda i, j: (i, j)))
            out_shapes.append(jax.ShapeDtypeStruct((t, n), dt))
    return pl.pallas_call(
        functools.partial(_proj_kernel, n_groups=n_groups, rope_flags=rope_flags, rope_half=rope_half,
                          out_scales=out_scales, n_alias=len(aliased)),
        out_shape=out_shapes,
        grid=(t // tm, steps),
        in_specs=in_specs,
        out_specs=out_specs,
        input_output_aliases={k: gi for k, (gi, _) in enumerate(aliased)},
        scratch_shapes=[pltpu.VMEM((tm, d), BF16)],
        compiler_params=_params(("arbitrary", "arbitrary")),
        name="proj",
    )(*args)


def _mla_q_kernel(*refs, rope, rope_half, q_scale):
    if rope:
        cq_ref, g_ref, w_ref, cos_ref, slo_ref, shi_ref, o_ref = refs
    else:
        cq_ref, g_ref, w_ref, o_ref = refs
    n = _rms(cq_ref[...], g_ref[...]).astype(BF16)
    q = jnp.dot(n, w_ref[...], preferred_element_type=F32) * q_scale
    if rope:
        cos, slo, shi = cos_ref[...], slo_ref[...], shi_ref[...]
        parts = []
        for s in range(0, q.shape[1], 2 * LANES):
            parts.append(q[:, s:s + LANES])
            parts.append(_rope(q[:, s + LANES:s + 2 * LANES], cos, slo, shi, rope_half))
        q = jnp.concatenate(parts, axis=1)
    o_ref[...] = q.astype(o_ref.dtype)


def _mla_q(cq, seq, g_q, w_uq_cat, q_scale, rope_tables=None, rope_half=0, tm=512):
    t, r = cq.shape
    n = w_uq_cat.shape[1]
    rope = rope_tables is not None
    tiles_per_seq = max(seq // tm, 1)
    in_specs = [
        pl.BlockSpec((tm, r), lambda i: (i, 0)),
        pl.BlockSpec((1, r), lambda i: (0, 0)),
        pl.BlockSpec((r, n), lambda i: (0, 0)),
    ]
    args = [cq, g_q.reshape(1, r), w_uq_cat]
    if rope:
        for tab in rope_tables:
            in_specs.append(pl.BlockSpec((tm, LANES), lambda i: (i % tiles_per_seq, 0)))
            args.append(tab)
    return pl.pallas_call(
        functools.partial(_mla_q_kernel, rope=rope, rope_half=rope_half, q_scale=q_scale),
        out_shape=jax.ShapeDtypeStruct((t, n), BF16),
        grid=(t // tm,),
        in_specs=in_specs,
        out_specs=pl.BlockSpec((tm, n), lambda i: (i, 0)),
        compiler_params=_params(("arbitrary",)),
        name="mla_q",
    )(*args)


def _mla_kv_kernel(*refs, normalise, rope, rope_half, emit_cache, n_alias, n_heads, nope):
    refs = list(refs[n_alias:])
    ckv_ref, kpe_ref = refs[:2]
    pos = 2
    if normalise:
        g_ref = refs[pos]
        pos += 1
    wk_ref, wv_ref = refs[pos:pos + 2]
    pos += 2
    if rope:
        cos_ref, slo_ref, shi_ref = refs[pos:pos + 3]
        pos += 3
    kcat_ref, v_ref = refs[pos:pos + 2]
    pos += 2
    ckv = ckv_ref[...]
    if normalise:
        ckv = _rms(ckv, g_ref[...])
    kpe = kpe_ref[...]
    if emit_cache:
        ckv_out_ref, kpe_out_ref = refs[pos:pos + 2]
        ckv_out_ref[...] = ckv.reshape(ckv_out_ref.shape)
        kpe_out_ref[...] = kpe[:, :kpe_out_ref.shape[-1]].reshape(kpe_out_ref.shape)
    if rope:
        kpe = _rope(kpe, cos_ref[...], slo_ref[...], shi_ref[...], rope_half)
    cb = ckv.astype(BF16)
    kn = jnp.dot(cb, wk_ref[...], preferred_element_type=F32).astype(BF16)
    v_ref[...] = jnp.dot(cb, wv_ref[...], preferred_element_type=F32).astype(BF16)
    kpe_b = kpe.astype(BF16)
    for h in range(n_heads):
        kcat_ref[:, h * 2 * nope:h * 2 * nope + nope] = kn[:, h * nope:(h + 1) * nope]
        kcat_ref[:, h * 2 * nope + nope:(h + 1) * 2 * nope] = kpe_b


def _mla_kv(ckv_arr, ckv_spec, kpe_arr, kpe_spec, t, seq, w_uk, w_uv, g_kv=None, rope_tables=None,
            rope_half=0, cache=None, kpe_width=64, tm=512):
    lora, nk = w_uk.shape
    nv = w_uv.shape[1]
    nope = LANES
    n_heads = nk // nope
    normalise = g_kv is not None
    rope = rope_tables is not None
    tiles_per_seq = max(seq // tm, 1)
    aliased = [] if cache is None or cache[2] is None else [cache[2], cache[3]]
    in_specs = [pl.BlockSpec(memory_space=pl.ANY) for _ in aliased] + [ckv_spec, kpe_spec]
    args = aliased + [ckv_arr, kpe_arr]
    if normalise:
        in_specs.append(pl.BlockSpec((1, lora), lambda i: (0, 0)))
        args.append(g_kv.reshape(1, lora))
    in_specs += [pl.BlockSpec((lora, nk), lambda i: (0, 0)), pl.BlockSpec((lora, nv), lambda i: (0, 0))]
    args += [w_uk, w_uv]
    if rope:
        for tab in rope_tables:
            in_specs.append(pl.BlockSpec((tm, LANES), lambda i: (i % tiles_per_seq, 0)))
            args.append(tab)
    out_shapes = [jax.ShapeDtypeStruct((t, 2 * nk), BF16), jax.ShapeDtypeStruct((t, nv), BF16)]
    out_specs = [pl.BlockSpec((tm, 2 * nk), lambda i: (i, 0)), pl.BlockSpec((tm, nv), lambda i: (i, 0))]
    if cache is not None:
        slot, n_slots = cache[:2]
        out_shapes += [jax.ShapeDtypeStruct((t // seq, n_slots, seq, lora), F32),
                       jax.ShapeDtypeStruct((t // seq, n_slots, seq, kpe_width), F32)]
        out_specs += [_slot_spec(tm // seq, seq, lora, slot, lambda i: 0),
                      _slot_spec(tm // seq, seq, kpe_width, slot, lambda i: 0)]
    return pl.pallas_call(
        functools.partial(_mla_kv_kernel, normalise=normalise, rope=rope, rope_half=rope_half,
                          emit_cache=cache is not None, n_alias=len(aliased), n_heads=n_heads, nope=nope),
        out_shape=out_shapes,
        grid=(t // tm,),
        in_specs=in_specs,
        out_specs=out_specs,
        input_output_aliases={k: 2 + k for k in range(len(aliased))},
        compiler_params=_params(("arbitrary",)),
        name="mla_kv",
    )(*args)


def _softmax_parts(s_list):
    m = functools.reduce(jnp.maximum, [jnp.max(s, axis=-1, keepdims=True) for s in s_list])
    p_list = [jnp.exp2(s - m) for s in s_list]
    l = functools.reduce(lambda a, b: a + b, [jnp.sum(p, axis=-1, keepdims=True) for p in p_list])
    return p_list, l


def _qk(q, k):
    return lax.dot_general(q, k, (((1,), (1,)), ((), ())), preferred_element_type=F32)


def _attn_kernel(*refs, n_seg, diff, lam_init, hps, dq, dv):
    q_ref = refs[0]
    kv_refs = refs[1:1 + 2 * n_seg]
    g_ref = refs[1 + 2 * n_seg]
    pos = 2 + 2 * n_seg
    if diff:
        lam_ref, gsub_ref = refs[pos:pos + 2]
        pos += 2
        lp = lam_ref[...]
        lam = (jnp.exp(jnp.sum(lp[0:1] * lp[1:2], axis=-1, keepdims=True))
               - jnp.exp(jnp.sum(lp[2:3] * lp[3:4], axis=-1, keepdims=True)) + lam_init)
    o_ref = refs[pos]
    for hh in range(hps):
        q = q_ref[:, hh * dq:(hh + 1) * dq].astype(BF16)
        ks = [kv_refs[2 * i][:, hh * dq:(hh + 1) * dq].astype(BF16) for i in range(n_seg)]
        vs = [kv_refs[2 * i + 1][:, hh * dv:(hh + 1) * dv].astype(BF16) for i in range(n_seg)]
        if not diff:
            p_list, l = _softmax_parts([_qk(q, k) for k in ks])
            o = functools.reduce(lambda a, b: a + b,
                                 [jnp.dot(p.astype(BF16), v, preferred_element_type=F32)
                                  for p, v in zip(p_list, vs)])
            o = o / l
        else:
            d = dq // 2
            p1, l1 = _softmax_parts([_qk(q[:, :d], k[:, :d]) for k in ks])
            p2, l2 = _softmax_parts([_qk(q[:, d:], k[:, d:]) for k in ks])
            a1 = 1.0 / l1
            a2 = lam / l2
            o = functools.reduce(lambda a, b: a + b,
                                 [jnp.dot((pa * a1 - pb * a2).astype(BF16), v, preferred_element_type=F32)
                                  for pa, pb, v in zip(p1, p2, vs)])
            o = _rms(o, gsub_ref[...]) * (1.0 - lam_init)
        g = g_ref[:, hh * dv:(hh + 1) * dv].astype(F32)
        o_ref[:, hh * dv:(hh + 1) * dv] = (o * _silu(g)).astype(o_ref.dtype)


def _attention(q, segs, gate, n_heads, dq, dv, tq, hps=1, diff=False, lam_params=None,
               g_sub=None, lam_init=0.0):
    b, sq, _ = q.shape
    in_specs = [pl.BlockSpec((None, tq, hps * dq), lambda bi, h, qi: (bi, qi, h))]
    args = [q]
    for k_arr, k_spec, v_arr, v_spec in segs:
        in_specs += [k_spec, v_spec]
        args += [k_arr, v_arr]
    in_specs.append(pl.BlockSpec((None, tq, hps * dv), lambda bi, h, qi: (bi, qi, h)))
    args.append(gate)
    if diff:
        in_specs += [pl.BlockSpec(lam_params.shape, lambda bi, h, qi: (0, 0)),
                     pl.BlockSpec((1, dv), lambda bi, h, qi: (0, 0))]
        args += [lam_params, g_sub.reshape(1, dv)]
    return pl.pallas_call(
        functools.partial(_attn_kernel, n_seg=len(segs), diff=diff, lam_init=lam_init,
                          hps=hps, dq=dq, dv=dv),
        out_shape=jax.ShapeDtypeStruct((b, sq, n_heads * dv), BF16),
        grid=(b, n_heads // hps, sq // tq),
        in_specs=in_specs,
        out_specs=pl.BlockSpec((None, tq, hps * dv), lambda bi, h, qi: (bi, qi, h)),
        compiler_params=_params(("arbitrary", "arbitrary", "arbitrary")),
        name="attn_diff" if diff else "attn",
    )(*args)


def _seg3(arr, sk, width):
    return arr, pl.BlockSpec((None, sk, width), lambda bi, h, qi: (bi, 0, h))


def _seg4(arr, layer, sk, width):
    return arr, pl.BlockSpec((None, None, sk, width), lambda bi, h, qi: (bi, layer, 0, h))


def _attn_latent_kernel(*refs, diff, lam_init, tq, dq, dv):
    q_ref, kc_ref, vc_ref, k_ref, v_ref, g_ref = refs[:6]
    pos = 6
    if diff:
        lam_ref, gsub_ref = refs[pos:pos + 2]
        pos += 2
        lp = lam_ref[...]
        lam = (jnp.exp(jnp.sum(lp[0:1] * lp[1:2], axis=-1, keepdims=True))
               - jnp.exp(jnp.sum(lp[2:3] * lp[3:4], axis=-1, keepdims=True)) + lam_init)
    o_ref = refs[pos]
    s_bufs = refs[pos + 1:pos + 3]
    m_bufs = refs[pos + 3:pos + 5]
    kcb_ref, vcb_ref = refs[pos + 5:pos + 7]
    n_maps = 2 if diff else 1
    d = dq // n_maps
    past = kc_ref.shape[0]
    n_blocks = q_ref.shape[0] // tq

    kcb_ref[...] = kc_ref[...].astype(BF16)
    vcb_ref[...] = vc_ref[...].astype(BF16)

    def scores(qi, slot):
        q = q_ref[pl.ds(pl.multiple_of(qi * tq, tq), tq), :]
        for mp in range(n_maps):
            qm = q[:, mp * d:(mp + 1) * d]
            s_c = _qk(qm, kcb_ref[:, mp * d:(mp + 1) * d])
            s_l = _qk(qm, k_ref[:, mp * d:(mp + 1) * d])
            s_bufs[slot][mp, :, 0:past] = s_c
            s_bufs[slot][mp, :, past:] = s_l
            m_bufs[slot][mp] = jnp.maximum(jnp.max(s_c, axis=-1, keepdims=True),
                                           jnp.max(s_l, axis=-1, keepdims=True))

    def finish(qi, slot):
        rows = pl.ds(pl.multiple_of(qi * tq, tq), tq)
        ps = [jnp.exp2(s_bufs[slot][mp] - m_bufs[slot][mp]) for mp in range(n_maps)]
        ls = [jnp.sum(p, axis=-1, keepdims=True) for p in ps]
        w = ps[0].astype(BF16)
        if diff:
            w = w - (lam * ls[0] / ls[1]).astype(BF16) * ps[1].astype(BF16)
        o = (jnp.dot(w[:, :past], vcb_ref[...], preferred_element_type=F32)
             + jnp.dot(w[:, past:], v_ref[...], preferred_element_type=F32)) * (1.0 / ls[0])
        if diff:
            o = _rms(o, gsub_ref[...]) * (1.0 - lam_init)
        o_ref[rows, :] = (o * _silu(g_ref[rows, :].astype(F32))).astype(o_ref.dtype)

    scores(0, 0)

    def body(u, carry):
        scores(2 * u + 1, 1)
        finish(2 * u, 0)
        scores(jnp.minimum(2 * u + 2, n_blocks - 1), 0)
        finish(2 * u + 1, 1)
        return carry

    lax.fori_loop(0, n_blocks // 2, body, 0)


def _attention_latent(q, kc, kc_spec, vc, vc_spec, k, v, gate, n_heads, dq, dv, tq=256, diff=False,
                      lam_params=None, g_sub=None, lam_init=0.0):
    b, s, _ = q.shape
    past = kc_spec.block_shape[-2]
    assert (s // tq) % 2 == 0
    n_maps = 2 if diff else 1
    head = lambda w: pl.BlockSpec((None, s, w), lambda bi, h: (bi, 0, h))
    in_specs = [head(dq), kc_spec, vc_spec, head(dq), head(dv), head(dv)]
    args = [q, kc, vc, k, v, gate]
    if diff:
        in_specs += [pl.BlockSpec(lam_params.shape, lambda bi, h: (0, 0)),
                     pl.BlockSpec((1, dv), lambda bi, h: (0, 0))]
        args += [lam_params, g_sub.reshape(1, dv)]
    return pl.pallas_call(
        functools.partial(_attn_latent_kernel, diff=diff, lam_init=lam_init, tq=tq,
                          dq=dq, dv=dv),
        out_shape=jax.ShapeDtypeStruct((b, s, n_heads * dv), BF16),
        grid=(b, n_heads),
        in_specs=in_specs,
        out_specs=head(dv),
        scratch_shapes=[pltpu.VMEM((n_maps, tq, past + s), F32), pltpu.VMEM((n_maps, tq, past + s), F32),
                        pltpu.VMEM((n_maps, tq, 1), F32), pltpu.VMEM((n_maps, tq, 1), F32),
                        pltpu.VMEM((past, dq), BF16), pltpu.VMEM((past, dv), BF16)],
        compiler_params=_params(("arbitrary", "arbitrary")),
        name="attn_latent_diff" if diff else "attn_latent",
    )(*args)


NA_Q_ROWS = 2
NA_K_ROWS = WIN_R + NA_Q_ROWS
NA_K_PAIRS = NA_K_ROWS * GRID_W // LANES
RPB_W = 2 * WIN_C - 1


def _na_blocks(rows):
    assert 2 * GRID_W == LANES and rows >= NA_K_ROWS
    wr = min(WIN_R, rows)
    r = np.arange(rows)
    rs = np.clip(r - wr // 2, 0, rows - wr)
    starts, variants, keys = [], [], {}
    for t in range(rows // NA_Q_ROWS):
        r0 = t * NA_Q_ROWS
        ws = int(np.clip(r0 - wr // 2, 0, rows - NA_K_ROWS))
        sig = (r0 - ws,) + tuple(int(rs[r0 + dq] - ws) for dq in range(NA_Q_ROWS))
        assert all(0 <= f and f + wr <= NA_K_ROWS for f in sig[1:])
        variants.append(keys.setdefault(sig, len(keys)))
        starts.append(ws)
    sigs = sorted(keys, key=keys.get)
    idx = np.zeros((len(sigs), NA_Q_ROWS, NA_K_PAIRS, LANES), np.int32)
    kind = np.ones(idx.shape, np.int32)
    for vi, sig in enumerate(sigs):
        for dq in range(NA_Q_ROWS):
            for j in range(NA_K_ROWS):
                lanes = slice((j % 2) * GRID_W, (j % 2) * GRID_W + RPB_W)
                if sig[1 + dq] <= j < sig[1 + dq] + wr:
                    dr = j - (sig[0] + dq)
                    idx[vi, dq, j // 2, lanes] = (dr + WIN_R - 1) * RPB_W + np.arange(RPB_W)
                    kind[vi, dq, j // 2, lanes] = 0
                else:
                    kind[vi, dq, j // 2, (j % 2) * GRID_W:(j % 2 + 1) * GRID_W] = 2
    return starts, variants, idx, kind


def _na_kernel(q_ref, k_ref, v_ref, kc_ref, vc_ref, rows_ref, g_ref, o_ref, bias_ref, *, starts, variants):
    nq = NA_Q_ROWS * GRID_W
    nk = NA_K_ROWS * GRID_W

    @pl.when(pl.program_id(1) == 0)
    def _():
        ci = lax.broadcasted_iota(jnp.int32, (GRID_W, LANES), 0)
        kc = lax.broadcasted_iota(jnp.int32, (GRID_W, LANES), 1) & (GRID_W - 1)
        cs = jnp.clip(ci - WIN_C // 2, 0, GRID_W - WIN_C)
        col_mask = jnp.where((kc >= cs) & (kc < cs + WIN_C), 0.0, NEG_BIG)
        for vi in range(bias_ref.shape[0]):
            for dq in range(NA_Q_ROWS):
                vecs = rows_ref[vi, dq]
                for jp in range(NA_K_PAIRS):
                    w = jnp.broadcast_to(vecs[jp:jp + 1, :], (GRID_W, LANES))
                    tile = pltpu.roll(w, LANES - (WIN_C - 1), 1, stride=1, stride_axis=0)
                    bias_ref[vi, dq * GRID_W:(dq + 1) * GRID_W, jp * LANES:(jp + 1) * LANES] = (
                        (tile + col_mask) * LOG2E)

    kc = kc_ref[...].astype(BF16)
    vc = vc_ref[...].astype(BF16)

    def body(t, carry):
        start = jnp.int32(starts[0])
        var = jnp.int32(variants[0])
        for ti in range(1, len(starts)):
            start = jnp.where(t == ti, jnp.int32(starts[ti]), start)
            var = jnp.where(t == ti, jnp.int32(variants[ti]), var)
        q0 = pl.multiple_of(t * nq, nq)
        k0 = pl.multiple_of(start * GRID_W, GRID_W)
        q = q_ref[pl.ds(q0, nq), :]
        kw = k_ref[pl.ds(k0, nk), :]
        vw = v_ref[pl.ds(k0, nk), :]
        s_w = _qk(q, kw) + bias_ref[var]
        s_c = _qk(q, kc)
        (p_w, p_c), l = _softmax_parts([s_w, s_c])
        o = (jnp.dot(p_w.astype(BF16), vw, preferred_element_type=F32)
             + jnp.dot(p_c.astype(BF16), vc, preferred_element_type=F32)) / l
        g = g_ref[pl.ds(q0, nq), :].astype(F32)
        o_ref[pl.ds(q0, nq), :] = (o * _silu(g)).astype(o_ref.dtype)
        return carry

    lax.fori_loop(0, len(starts), body, 0, unroll=4)


def _na_latent(q, k, v, cache_k, cache_v, layer, bias_rows, gate, n_heads, d, starts, variants):
    b, s, _ = q.shape
    past = cache_k.shape[2]
    n_var = bias_rows.shape[1]
    head = pl.BlockSpec((None, s, d), lambda h, bi: (bi, 0, h))
    ctx = pl.BlockSpec((None, None, past, d), lambda h, bi: (bi, layer, 0, h))
    return pl.pallas_call(
        functools.partial(_na_kernel, starts=tuple(starts), variants=tuple(variants)),
        out_shape=jax.ShapeDtypeStruct((b, s, n_heads * d), BF16),
        grid=(n_heads, b),
        in_specs=[head, head, head, ctx, ctx,
                  pl.BlockSpec((None,) + bias_rows.shape[1:], lambda h, bi: (h, 0, 0, 0, 0)),
                  head],
        out_specs=head,
        scratch_shapes=[pltpu.VMEM((n_var, NA_Q_ROWS * GRID_W, NA_K_ROWS * GRID_W), F32)],
        compiler_params=_params(("arbitrary", "arbitrary")),
        name="na_latent",
    )(q, k, v, cache_k, cache_v, bias_rows, gate)


def _merge_kernel(*refs, n_in):
    o_refs = refs[:n_in]
    w_ref, x_ref, gate_ref, g_ref, out_ref = refs[n_in:]
    y = None
    row = 0
    for o_ref in o_refs:
        width = o_ref.shape[1]
        part = jnp.dot(o_ref[...], w_ref[row:row + width, :], preferred_element_type=F32)
        y = part if y is None else y + part
        row += width
    out_ref[...] = x_ref[...] + gate_ref[...] * _rms(y, g_ref[...])


def _merge(o_list, w_out, x, ada_l, mod_row, g_post, tm=512):
    t, d = x.shape
    in_specs = [pl.BlockSpec((tm, o.shape[1]), lambda i: (i, 0)) for o in o_list]
    in_specs += [
        pl.BlockSpec(w_out.shape, lambda i: (0, 0)),
        pl.BlockSpec((tm, d), lambda i: (i, 0)),
        pl.BlockSpec((None, 1, d), lambda i: (mod_row(i), 0, 2)),
        pl.BlockSpec((1, d), lambda i: (0, 0)),
    ]
    return pl.pallas_call(
        functools.partial(_merge_kernel, n_in=len(o_list)),
        out_shape=jax.ShapeDtypeStruct((t, d), F32),
        grid=(t // tm,),
        in_specs=in_specs,
        out_specs=pl.BlockSpec((tm, d), lambda i: (i, 0)),
        compiler_params=_params(("arbitrary",)),
        name="merge",
    )(*o_list, w_out, x, ada_l, g_post.reshape(1, d))


def _rope_tables(s, n):
    a = n // 2
    half = a // 2
    t = jnp.arange(s)
    inv = ROPE_BASE ** (-jnp.arange(half, dtype=F32) / half)
    ang_r = (t // GRID_W).astype(F32)[:, None] * inv[None, :]
    ang_c = (t % GRID_W).astype(F32)[:, None] * inv[None, :]
    cos = jnp.concatenate([jnp.cos(ang_r)] * 2 + [jnp.cos(ang_c)] * 2, axis=-1)
    sin = jnp.concatenate([jnp.sin(ang_r)] * 2 + [jnp.sin(ang_c)] * 2, axis=-1)
    low = (np.arange(n) % a) < half
    sin_lo = jnp.where(low[None, :], -sin, 0.0)
    sin_hi = jnp.where(low[None, :], 0.0, sin)
    pad = LANES - n
    if pad:
        cos = jnp.pad(cos, ((0, 0), (0, pad)), constant_values=1.0)
        sin_lo = jnp.pad(sin_lo, ((0, 0), (0, pad)))
        sin_hi = jnp.pad(sin_hi, ((0, 0), (0, pad)))
    return (cos, sin_lo, sin_hi), half


def kernel(x_prompt, x_sample, cache_na_k, cache_na_v, cache_mla_ckv, cache_mla_kpe, cache_diff_k, cache_diff_v, c, c_ctx, w_ada, b_ada, g_pre, g_post, w_in_even, w_out_even, na_rpb, mla_g_q, mla_w_uq, mla_g_kv, mla_w_ukv, w_in_odd, w_out_odd, diff_lambda, diff_g):
    bp, sp, d = x_prompt.shape
    bs, ss, _ = x_sample.shape
    depth = w_ada.shape[0]
    n_even, n_odd = w_in_even.shape[0], w_in_odd.shape[0]
    past = cache_na_k.shape[2]
    na_heads, na_dim = cache_na_k.shape[3:]
    na_w = na_heads * na_dim
    kv_lora = cache_mla_ckv.shape[3]
    rope_w = cache_mla_kpe.shape[3]
    q_lora = mla_g_q.shape[1]
    diff_heads = cache_diff_k.shape[3]
    diff_w = diff_heads * cache_diff_k.shape[4]
    diff_d = diff_w // (2 * diff_heads)
    mla_w = w_out_even.shape[1] - na_w
    mla_v = mla_nope = LANES
    mla_heads = mla_w // mla_v
    na_scale = na_dim ** -0.5
    mla_scale = (mla_nope + rope_w) ** -0.5
    diff_scale = diff_d ** -0.5
    tp, ts = bp * sp, bs * ss
    tm = 512
    even_scales = (na_scale * LOG2E, 1.0, 1.0, 1.0, 1.0, 1.0, 1.0)
    odd_scales = (diff_scale * LOG2E, 1.0, 1.0, 1.0)

    ctx_row = bs
    cond = jnp.concatenate([c, c_ctx[None, :], jnp.zeros((16 - bs - 1, d), F32)], axis=0)
    ada = _ada(cond, w_ada, b_ada)
    row_p = lambda i: ctx_row
    row_s = lambda i: (i * tm) // ss

    rope_diff, half_diff = _rope_tables(ss, diff_d)
    rope_mla, half_mla = _rope_tables(ss, rope_w)

    starts, variants, rows_idx, rows_kind = _na_blocks(ss // GRID_W)

    xp = x_prompt.reshape(tp, d)
    xs = x_sample.reshape(ts, d)
    ck_na = cache_na_k.reshape(bs, -1, past, na_w)
    cv_na = cache_na_v.reshape(bs, -1, past, na_w)
    ck_diff = cache_diff_k.reshape(bs, -1, past, diff_w)
    cv_diff = cache_diff_v.reshape(bs, -1, past, diff_w)
    cache_kpe_pad = jnp.pad(cache_mla_kpe, ((0, 0), (0, 0), (0, 0), (0, LANES - rope_w)))

    new_na_k = new_na_v = new_ckv = new_kpe = new_dk = new_dv = None
    for l in range(depth):
        i = l // 2
        ada_l = ada[l].reshape(16, 1, 3 * d)
        if l % 2 == 0:
            w_in = w_in_even[i]
            o = 0
            cols = {}
            for name, width in (("qa", na_w), ("ka", na_w), ("va", na_w), ("ga", na_w), ("cq", q_lora),
                                ("ckv", kv_lora), ("kpe", rope_w), ("gb", mla_w)):
                cols[name] = w_in[:, o:o + width]
                o += width
            w_ckvkpe = jnp.concatenate(
                [cols["ckv"], cols["kpe"], jnp.zeros((d, 512 - kv_lora - rope_w), F32)], axis=1)
            weights = [cols["qa"], cols["ka"], cols["va"], cols["ga"], cols["cq"], w_ckvkpe, cols["gb"]]
            weights = [w.astype(BF16) for w in weights]
            w_uq = mla_w_uq[i].reshape(q_lora, mla_heads, mla_nope + rope_w)
            w_uq_cat = jnp.pad(w_uq, ((0, 0), (0, 0), (0, 2 * LANES - mla_nope - rope_w)))
            w_uq_cat = w_uq_cat.reshape(q_lora, mla_heads * 2 * LANES).astype(BF16)
            w_ukv = mla_w_ukv[i].reshape(kv_lora, mla_heads, mla_nope + mla_v)
            w_uk = w_ukv[:, :, :mla_nope].reshape(kv_lora, mla_heads * mla_nope).astype(BF16)
            w_uv = w_ukv[:, :, mla_nope:].reshape(kv_lora, mla_heads * mla_v).astype(BF16)
            w_out = w_out_even[i].astype(BF16)
            rpb_flat = na_rpb[i].reshape(na_heads, -1)
            bias_rows = jnp.where(rows_kind[None] == 0, rpb_flat[:, rows_idx],
                                  jnp.where(rows_kind[None] == 1, 0.0, NEG_BIG))

            qa, new_na_k, new_na_v, ga, cq, ckvkpe, gb = _proj(
                xp, sp, g_pre[l], ada_l, row_p, weights, [BF16, F32, F32, BF16, F32, F32, BF16], steps=4,
                out_scales=even_scales, cache_slots={1: (i, n_even, new_na_k), 2: (i, n_even, new_na_v)})
            q_cat = _mla_q(cq, sp, mla_g_q[i], w_uq_cat, mla_scale * LOG2E)
            k_cat, v_mla, new_ckv, new_kpe = _mla_kv(
                ckvkpe, pl.BlockSpec((tm, kv_lora), lambda i_: (i_, 0)),
                ckvkpe, pl.BlockSpec((tm, LANES), lambda i_: (i_, kv_lora // LANES)),
                tp, sp, w_uk, w_uv, g_kv=mla_g_kv[i], cache=(i, n_even, new_ckv, new_kpe), kpe_width=rope_w)
            r3 = lambda a: a.reshape(bp, sp, -1)
            oa = _attention(r3(qa), [_seg4(new_na_k, i, sp, na_w) + _seg4(new_na_v, i, sp, na_w)], r3(ga),
                            na_heads, na_dim, na_dim, tq=sp, hps=na_heads)
            ob = _attention(r3(q_cat), [_seg3(r3(k_cat), sp, mla_heads * 2 * LANES)
                                        + _seg3(r3(v_mla), sp, mla_w)],
                            r3(gb), mla_heads, 2 * LANES, mla_v, tq=sp, hps=mla_heads)
            xp = _merge([oa.reshape(tp, -1), ob.reshape(tp, -1)], w_out, xp, ada_l, row_p, g_post[l])

            qa, ka, va, ga, cq, ckvkpe, gb = _proj(
                xs, ss, g_pre[l], ada_l, row_s, weights, [BF16, BF16, BF16, BF16, F32, F32, BF16], steps=4,
                out_scales=even_scales)
            q_cat = _mla_q(cq, ss, mla_g_q[i], w_uq_cat, mla_scale * LOG2E, rope_tables=rope_mla,
                           rope_half=half_mla)
            k_cat, v_mla = _mla_kv(
                ckvkpe, pl.BlockSpec((tm, kv_lora), lambda i_: (i_, 0)),
                ckvkpe, pl.BlockSpec((tm, LANES), lambda i_: (i_, kv_lora // LANES)),
                ts, ss, w_uk, w_uv, g_kv=mla_g_kv[i], rope_tables=rope_mla, rope_half=half_mla)
            tiles_past = past // tm
            kc_cat, vc_mla = _mla_kv(
                cache_mla_ckv, pl.BlockSpec((None, None, tm, kv_lora),
                                            lambda i_: (i_ // tiles_past, i, i_ % tiles_past, 0)),
                cache_kpe_pad, pl.BlockSpec((None, None, tm, LANES),
                                            lambda i_: (i_ // tiles_past, i, i_ % tiles_past, 0)),
                bs * past, past, w_uk, w_uv)
            r3 = lambda a: a.reshape(bs, ss, -1)
            oa = _na_latent(r3(qa), r3(ka), r3(va), ck_na, cv_na, i, bias_rows, r3(ga), na_heads, na_dim,
                            starts, variants)
            c3 = lambda a: a.reshape(bs, past, -1)
            ob = _attention_latent(
                r3(q_cat),
                c3(kc_cat), pl.BlockSpec((None, past, 2 * LANES), lambda bi, h: (bi, 0, h)),
                c3(vc_mla), pl.BlockSpec((None, past, mla_v), lambda bi, h: (bi, 0, h)),
                r3(k_cat), r3(v_mla), r3(gb), mla_heads, 2 * LANES, mla_v)
            xs = _merge([oa.reshape(ts, -1), ob.reshape(ts, -1)], w_out, xs, ada_l, row_s, g_post[l])
        else:
            lam_init = 0.8 - 0.6 * math.exp(-0.3 * l)
            w_in = w_in_odd[i].astype(BF16)
            weights = [w_in[:, k * diff_w:(k + 1) * diff_w] for k in range(4)]
            w_out = w_out_odd[i].astype(BF16)
            dh = 2 * diff_d

            q, new_dk, new_dv, g = _proj(
                xp, sp, g_pre[l], ada_l, row_p, weights, [BF16, F32, F32, BF16], steps=4,
                out_scales=odd_scales, cache_slots={1: (i, n_odd, new_dk), 2: (i, n_odd, new_dv)})
            r3 = lambda a: a.reshape(bp, sp, -1)
            o = _attention(r3(q), [_seg4(new_dk, i, sp, diff_w) + _seg4(new_dv, i, sp, diff_w)], r3(g),
                           diff_heads, dh, dh, tq=sp, hps=diff_heads, diff=True,
                           lam_params=diff_lambda[i], g_sub=diff_g[i], lam_init=lam_init)
            xp = _merge([o.reshape(tp, -1)], w_out, xp, ada_l, row_p, g_post[l])

            q, k, v, g = _proj(xs, ss, g_pre[l], ada_l, row_s, weights, [BF16, BF16, BF16, BF16], steps=4,
                               rope_flags=(True, True, False, False), rope_tables=rope_diff,
                               rope_half=half_diff, out_scales=odd_scales)
            r3 = lambda a: a.reshape(bs, ss, -1)
            ctx_spec = pl.BlockSpec((None, None, past, dh), lambda bi, h: (bi, i, 0, h))
            o = _attention_latent(
                r3(q), ck_diff, ctx_spec, cv_diff, ctx_spec, r3(k), r3(v), r3(g), diff_heads, dh, dh,
                diff=True, lam_params=diff_lambda[i], g_sub=diff_g[i], lam_init=lam_init)
            xs = _merge([o.reshape(ts, -1)], w_out, xs, ada_l, row_s, g_post[l])

    return (xp.reshape(bp, sp, d), xs.reshape(bs, ss, d),
            new_na_k.reshape(bp, n_even, sp, na_heads, na_dim),
            new_na_v.reshape(bp, n_even, sp, na_heads, na_dim),
            new_ckv, new_kpe,
            new_dk.reshape(bp, n_odd, sp, diff_heads, 2 * diff_d),
            new_dv.reshape(bp, n_odd, sp, diff_heads, 2 * diff_d))
```

```python
import functools
import math

import numpy as np
import jax
import jax.numpy as jnp
from jax import lax
from jax.experimental import pallas as pl
from jax.experimental.pallas import tpu as pltpu

GRID_W = 64
WIN_R = 8
WIN_C = 16
EPS = 1e-6
ROPE_BASE = 10000.0
LOG2E = 1.4426950408889634
NEG_BIG = -1e30

LANES = 128
VMEM_LIMIT = 56 * 1024 * 1024

F32 = jnp.float32
BF16 = jnp.bfloat16


def _params(sem, vmem=VMEM_LIMIT):
    return pltpu.CompilerParams(dimension_semantics=sem, vmem_limit_bytes=vmem)


def _rms(x, g):
    return x * lax.rsqrt(jnp.mean(x * x, axis=-1, keepdims=True) + EPS) * g


def _silu(x):
    return x * (1.0 / (1.0 + jnp.exp(-x)))


def _rope(x, cos, sin_lo, sin_hi, half):
    return (x * cos + pltpu.roll(x, LANES - half, 1) * sin_lo
            + pltpu.roll(x, half, 1) * sin_hi)


def _slot_spec(rows, seq, width, slot, col_of):
    return pl.BlockSpec((rows, None, seq, width),
                        lambda *g: (g[0], slot, 0, col_of(*g)))


def _ada_kernel(cond_ref, w_ref, b_ref, o_ref):
    a = _silu(cond_ref[...]).astype(BF16)
    o_ref[...] = jnp.dot(a, w_ref[...].astype(BF16), preferred_element_type=F32) + b_ref[...]


def _ada(cond, w_ada, b_ada):
    depth, d, n = w_ada.shape
    rows = cond.shape[0]
    tn = 1024
    return pl.pallas_call(
        _ada_kernel,
        out_shape=jax.ShapeDtypeStruct((depth, rows, n), F32),
        grid=(depth, n // tn),
        in_specs=[
            pl.BlockSpec((rows, d), lambda l, j: (0, 0)),
            pl.BlockSpec((None, d, tn), lambda l, j: (l, 0, j)),
            pl.BlockSpec((None, 1, tn), lambda l, j: (l, 0, j)),
        ],
        out_specs=pl.BlockSpec((None, rows, tn), lambda l, j: (l, 0, j)),
        compiler_params=_params(("arbitrary", "arbitrary")),
        name="ada",
    )(cond, w_ada, b_ada.reshape(depth, 1, n))


def _proj_kernel(*refs, n_groups, rope_flags, rope_half, out_scales, n_alias, cache_heads, steps):
    refs = refs[n_alias:]
    x0_ref, xn_ref, g_ref, sh0_ref, sc0_ref, shn_ref, scn_ref = refs[:7]
    pos = 7
    if any(rope_flags):
        cos_ref, slo_ref, shi_ref = refs[pos:pos + 3]
        pos += 3
    w_refs = refs[pos:pos + n_groups]
    o_refs = refs[pos + n_groups:pos + 2 * n_groups]
    pos += 2 * n_groups
    c_refs = dict(zip(sorted(cache_heads), refs[pos:pos + len(cache_heads)]))
    h_ref = refs[pos + len(cache_heads)]
    i, j = pl.program_id(0), pl.program_id(1)
    slot = i % 2

    def modulated(x, sh_ref_, sc_ref_):
        return (_rms(x, g_ref[...]) * (1.0 + sc_ref_[...]) + sh_ref_[...]).astype(BF16)

    @pl.when((i == 0) & (j == 0))
    def _():
        h_ref[0] = modulated(x0_ref[...], sh0_ref, sc0_ref)

    h = h_ref[slot]
    for gi in range(n_groups):
        acc = jnp.dot(h, w_refs[gi][...], preferred_element_type=F32)
        if rope_flags[gi]:
            cos, slo, shi = cos_ref[...], slo_ref[...], shi_ref[...]
            parts = [_rope(acc[:, s:s + LANES], cos, slo, shi, rope_half)
                     for s in range(0, acc.shape[1], LANES)]
            acc = jnp.concatenate(parts, axis=1)
        if out_scales[gi] != 1.0:
            acc = acc * out_scales[gi]
        o_refs[gi][...] = acc.astype(o_refs[gi].dtype).reshape(o_refs[gi].shape)
        if gi in c_refs:
            c_ref, heads = c_refs[gi], cache_heads[gi]
            seq = c_ref.shape[1] // heads
            hw = c_ref.shape[2]
            for hl in range(acc.shape[1] // hw):
                hd = j * (acc.shape[1] // hw) + hl
                for bi in range(c_ref.shape[0]):
                    c_ref[bi, pl.ds(hd, seq, stride=heads), :] = acc[bi * seq:(bi + 1) * seq,
                                                                       hl * hw:(hl + 1) * hw]

    rows = h_ref.shape[1] // steps
    r0 = pl.multiple_of(j * rows, rows)
    h_ref[1 - slot, pl.ds(r0, rows), :] = modulated(xn_ref[pl.ds(r0, rows), :], shn_ref, scn_ref)


def _proj(x, seq, g_pre, ada_l, mod_row, weights, out_dtypes, steps, rope_flags=None,
          rope_tables=None, rope_half=0, out_scales=None, cache_slots=None, tm=512):
    t, d = x.shape
    n_groups = len(weights)
    n_tiles = t // tm
    rope_flags = tuple(rope_flags) if rope_flags is not None else (False,) * n_groups
    out_scales = tuple(out_scales) if out_scales is not None else (1.0,) * n_groups
    cache_slots = cache_slots or {}
    tiles_per_seq = max(seq // tm, 1)
    nxt = lambda i: jnp.minimum(i + 1, n_tiles - 1)
    aliased = [(gi, cs[2]) for gi, cs in sorted(cache_slots.items()) if cs[2] is not None]
    in_specs = [pl.BlockSpec(memory_space=pl.ANY) for _ in aliased]
    in_specs += [
        pl.BlockSpec((tm, d), lambda i, j: (0, 0)),
        pl.BlockSpec((tm, d), lambda i, j: (nxt(i), 0)),
        pl.BlockSpec((1, d), lambda i, j: (0, 0)),
        pl.BlockSpec((None, 1, d), lambda i, j: (mod_row(0), 0, 0)),
        pl.BlockSpec((None, 1, d), lambda i, j: (mod_row(0), 0, 1)),
        pl.BlockSpec((None, 1, d), lambda i, j: (mod_row(nxt(i)), 0, 0)),
        pl.BlockSpec((None, 1, d), lambda i, j: (mod_row(nxt(i)), 0, 1)),
    ]
    args = [prev for _, prev in aliased] + [x, x, g_pre.reshape(1, d), ada_l, ada_l, ada_l, ada_l]
    if any(rope_flags):
        for tab in rope_tables:
            in_specs.append(pl.BlockSpec((tm, LANES), lambda i, j: (i % tiles_per_seq, 0)))
            args.append(tab)
    out_specs, out_shapes = [], []
    for gi, ((w, col0, n), dt) in enumerate(zip(weights, out_dtypes)):
        tn = n // steps
        in_specs.append(pl.BlockSpec((d, tn), functools.partial(lambda i, j, b0: (0, b0 + j), b0=col0 // tn)))
        args.append(w)
        if gi in cache_slots and not cache_slots[gi][3]:
            slot, n_slots = cache_slots[gi][:2]
            out_specs.append(_slot_spec(tm // seq, seq, tn, slot, lambda i, j: j))
            out_shapes.append(jax.ShapeDtypeStruct((t // seq, n_slots, seq, n), dt))
        else:
            out_specs.append(pl.BlockSpec((tm, tn), lambda i, j: (i, j)))
            out_shapes.append(jax.ShapeDtypeStruct((t, n), dt))
    merged = [gi for gi in sorted(cache_slots) if cache_slots[gi][3]]
    for gi in merged:
        slot, n_slots, _, heads = cache_slots[gi]
        n = weights[gi][2]
        out_specs.append(_slot_spec(tm // seq, seq * heads, n // heads, slot, lambda i, j: 0))
        out_shapes.append(jax.ShapeDtypeStruct((t // seq, n_slots, seq * heads, n // heads), F32))
    cache_out = lambda gi: n_groups + merged.index(gi) if gi in merged else gi
    return pl.pallas_call(
        functools.partial(_proj_kernel, n_groups=n_groups, rope_flags=rope_flags, rope_half=rope_half,
                          out_scales=out_scales, n_alias=len(aliased),
                          cache_heads={gi: cache_slots[gi][3] for gi in merged}, steps=steps),
        out_shape=out_shapes,
        grid=(n_tiles, steps),
        in_specs=in_specs,
        out_specs=out_specs,
        input_output_aliases={k: cache_out(gi) for k, (gi, _) in enumerate(aliased)},
        scratch_shapes=[pltpu.VMEM((2, tm, d), BF16)],
        compiler_params=_params(("arbitrary", "arbitrary")),
        name="proj",
    )(*args)


def _mla_q_kernel(*refs, rope, rope_half, q_scale):
    if rope:
        cq_ref, g_ref, w_ref, cos_ref, slo_ref, shi_ref, o_ref = refs
    else:
        cq_ref, g_ref, w_ref, o_ref = refs
    n = _rms(cq_ref[...], g_ref[...]).astype(BF16)
    q = jnp.dot(n, w_ref[...], preferred_element_type=F32) * q_scale
    if rope:
        cos, slo, shi = cos_ref[...], slo_ref[...], shi_ref[...]
        parts = []
        for s in range(0, q.shape[1], 2 * LANES):
            parts.append(q[:, s:s + LANES])
            parts.append(_rope(q[:, s + LANES:s + 2 * LANES], cos, slo, shi, rope_half))
        q = jnp.concatenate(parts, axis=1)
    o_ref[...] = q.astype(o_ref.dtype)


def _mla_q(cq, seq, g_q, w_uq_cat, q_scale, rope_tables=None, rope_half=0, tm=512):
    t, r = cq.shape
    n = w_uq_cat.shape[1]
    rope = rope_tables is not None
    tiles_per_seq = max(seq // tm, 1)
    in_specs = [
        pl.BlockSpec((tm, r), lambda i: (i, 0)),
        pl.BlockSpec((1, r), lambda i: (0, 0)),
        pl.BlockSpec((r, n), lambda i: (0, 0)),
    ]
    args = [cq, g_q.reshape(1, r), w_uq_cat]
    if rope:
        for tab in rope_tables:
            in_specs.append(pl.BlockSpec((tm, LANES), lambda i: (i % tiles_per_seq, 0)))
            args.append(tab)
    return pl.pallas_call(
        functools.partial(_mla_q_kernel, rope=rope, rope_half=rope_half, q_scale=q_scale),
        out_shape=jax.ShapeDtypeStruct((t, n), BF16),
        grid=(t // tm,),
        in_specs=in_specs,
        out_specs=pl.BlockSpec((tm, n), lambda i: (i, 0)),
        compiler_params=_params(("arbitrary",)),
        name="mla_q",
    )(*args)


def _mla_kv_kernel(*refs, normalise, rope, rope_half, emit_cache, n_alias, n_heads, nope):
    refs = list(refs[n_alias:])
    ckv_ref, kpe_ref = refs[:2]
    pos = 2
    if normalise:
        g_ref = refs[pos]
        pos += 1
    wk_ref, wv_ref = refs[pos:pos + 2]
    pos += 2
    if rope:
        cos_ref, slo_ref, shi_ref = refs[pos:pos + 3]
        pos += 3
    kcat_ref, v_ref = refs[pos:pos + 2]
    pos += 2
    ckv = ckv_ref[...]
    if normalise:
        ckv = _rms(ckv, g_ref[...])
    kpe = kpe_ref[...]
    if emit_cache:
        ckv_out_ref, kpe_out_ref = refs[pos:pos + 2]
        ckv_out_ref[...] = ckv.reshape(ckv_out_ref.shape)
        kpe_out_ref[...] = kpe[:, :kpe_out_ref.shape[-1]].reshape(kpe_out_ref.shape)
    if rope:
        kpe = _rope(kpe, cos_ref[...], slo_ref[...], shi_ref[...], rope_half)
    cb = ckv.astype(BF16)
    kn = jnp.dot(cb, wk_ref[...], preferred_element_type=F32).astype(BF16)
    v_ref[...] = jnp.dot(cb, wv_ref[...], preferred_element_type=F32).astype(BF16)
    kpe_b = kpe.astype(BF16)
    for h in range(n_heads):
        kcat_ref[:, h * 2 * nope:h * 2 * nope + nope] = kn[:, h * nope:(h + 1) * nope]
        kcat_ref[:, h * 2 * nope + nope:(h + 1) * 2 * nope] = kpe_b


def _mla_kv(ckv_arr, ckv_spec, kpe_arr, kpe_spec, t, seq, w_uk, w_uv, g_kv=None, rope_tables=None,
            rope_half=0, cache=None, kpe_width=64, tm=512):
    lora, nk = w_uk.shape
    nv = w_uv.shape[1]
    nope = LANES
    n_heads = nk // nope
    normalise = g_kv is not None
    rope = rope_tables is not None
    tiles_per_seq = max(seq // tm, 1)
    aliased = [] if cache is None or cache[2] is None else [cache[2], cache[3]]
    in_specs = [pl.BlockSpec(memory_space=pl.ANY) for _ in aliased] + [ckv_spec, kpe_spec]
    args = aliased + [ckv_arr, kpe_arr]
    if normalise:
        in_specs.append(pl.BlockSpec((1, lora), lambda i: (0, 0)))
        args.append(g_kv.reshape(1, lora))
    in_specs += [pl.BlockSpec((lora, nk), lambda i: (0, 0)), pl.BlockSpec((lora, nv), lambda i: (0, 0))]
    args += [w_uk, w_uv]
    if rope:
        for tab in rope_tables:
            in_specs.append(pl.BlockSpec((tm, LANES), lambda i: (i % tiles_per_seq, 0)))
            args.append(tab)
    out_shapes = [jax.ShapeDtypeStruct((t, 2 * nk), BF16), jax.ShapeDtypeStruct((t, nv), BF16)]
    out_specs = [pl.BlockSpec((tm, 2 * nk), lambda i: (i, 0)), pl.BlockSpec((tm, nv), lambda i: (i, 0))]
    if cache is not None:
        slot, n_slots = cache[:2]
        out_shapes += [jax.ShapeDtypeStruct((t // seq, n_slots, seq, lora), F32),
                       jax.ShapeDtypeStruct((t // seq, n_slots, seq, kpe_width), F32)]
        out_specs += [_slot_spec(tm // seq, seq, lora, slot, lambda i: 0),
                      _slot_spec(tm // seq, seq, kpe_width, slot, lambda i: 0)]
    return pl.pallas_call(
        functools.partial(_mla_kv_kernel, normalise=normalise, rope=rope, rope_half=rope_half,
                          emit_cache=cache is not None, n_alias=len(aliased), n_heads=n_heads, nope=nope),
        out_shape=out_shapes,
        grid=(t // tm,),
        in_specs=in_specs,
        out_specs=out_specs,
        input_output_aliases={k: 2 + k for k in range(len(aliased))},
        compiler_params=_params(("arbitrary",)),
        name="mla_kv",
    )(*args)


def _softmax_parts(s_list):
    m = functools.reduce(jnp.maximum, [jnp.max(s, axis=-1, keepdims=True) for s in s_list])
    p_list = [jnp.exp2(s - m) for s in s_list]
    l = functools.reduce(lambda a, b: a + b, [jnp.sum(p, axis=-1, keepdims=True) for p in p_list])
    return p_list, l


def _qk(q, k):
    return lax.dot_general(q, k, (((1,), (1,)), ((), ())), preferred_element_type=F32)


def _diff_lambda(lam_ref, lam_init):
    lp = lam_ref[...]
    return (jnp.exp(jnp.sum(lp[0:1] * lp[1:2], axis=-1, keepdims=True))
            - jnp.exp(jnp.sum(lp[2:3] * lp[3:4], axis=-1, keepdims=True)) + lam_init)


def _softmax_weights(s_ref, m_ref, lam):
    ps = [jnp.exp2(s_ref[mp] - m_ref[mp]) for mp in range(s_ref.shape[0])]
    ls = [jnp.sum(p, axis=-1, keepdims=True) for p in ps]
    w = ps[0].astype(BF16)
    if len(ps) == 2:
        w = w - (lam * ls[0] / ls[1]).astype(BF16) * ps[1].astype(BF16)
    return w, 1.0 / ls[0]


def _attn_kernel(*refs, diff, lam_init, hps, dq, dv):
    q_ref, k_ref, v_ref, g_ref = refs[:4]
    pos = 4
    if diff:
        lam_ref, gsub_ref = refs[pos:pos + 2]
        pos += 2
        lam = _diff_lambda(lam_ref, lam_init)
    o_ref = refs[pos]
    for hh in range(hps):
        q = q_ref[:, hh * dq:(hh + 1) * dq].astype(BF16)
        k = k_ref[:, hh * dq:(hh + 1) * dq].astype(BF16)
        v = v_ref[:, hh * dv:(hh + 1) * dv].astype(BF16)
        if not diff:
            (p,), l = _softmax_parts([_qk(q, k)])
            o = jnp.dot(p.astype(BF16), v, preferred_element_type=F32) / l
        else:
            d = dq // 2
            (p1,), l1 = _softmax_parts([_qk(q[:, :d], k[:, :d])])
            (p2,), l2 = _softmax_parts([_qk(q[:, d:], k[:, d:])])
            w = p1 * (1.0 / l1) - p2 * (lam / l2)
            o = jnp.dot(w.astype(BF16), v, preferred_element_type=F32)
            o = _rms(o, gsub_ref[...]) * (1.0 - lam_init)
        g = g_ref[:, hh * dv:(hh + 1) * dv].astype(F32)
        o_ref[:, hh * dv:(hh + 1) * dv] = (o * _silu(g)).astype(o_ref.dtype)


def _attention(q, seg, gate, n_heads, dq, dv, tq, hps=1, diff=False, lam_params=None,
               g_sub=None, lam_init=0.0):
    b, sq, _ = q.shape
    k_arr, k_spec, v_arr, v_spec = seg
    in_specs = [pl.BlockSpec((None, tq, hps * dq), lambda bi, h, qi: (bi, qi, h)), k_spec, v_spec,
                pl.BlockSpec((None, tq, hps * dv), lambda bi, h, qi: (bi, qi, h))]
    args = [q, k_arr, v_arr, gate]
    if diff:
        in_specs += [pl.BlockSpec(lam_params.shape, lambda bi, h, qi: (0, 0)),
                     pl.BlockSpec((1, dv), lambda bi, h, qi: (0, 0))]
        args += [lam_params, g_sub.reshape(1, dv)]
    return pl.pallas_call(
        functools.partial(_attn_kernel, diff=diff, lam_init=lam_init, hps=hps, dq=dq, dv=dv),
        out_shape=jax.ShapeDtypeStruct((b, sq, n_heads * dv), BF16),
        grid=(b, n_heads // hps, sq // tq),
        in_specs=in_specs,
        out_specs=pl.BlockSpec((None, tq, hps * dv), lambda bi, h, qi: (bi, qi, h)),
        compiler_params=_params(("arbitrary", "arbitrary", "arbitrary")),
        name="attn_diff" if diff else "attn",
    )(*args)


def _seg3(arr, sk, width):
    return arr, pl.BlockSpec((None, sk, width), lambda bi, h, qi: (bi, 0, h))


def _seg4(arr, layer, sk, width):
    return arr, pl.BlockSpec((None, None, sk, width), lambda bi, h, qi: (bi, layer, 0, h))


def _attn_latent_kernel(*refs, diff, lam_init, tq, dq, dv):
    q_ref, kc_ref, vc_ref, k_ref, v_ref, g_ref = refs[:6]
    pos = 6
    lam = None
    if diff:
        lam_ref, gsub_ref = refs[pos:pos + 2]
        pos += 2
        lam = _diff_lambda(lam_ref, lam_init)
    o_ref = refs[pos]
    s_bufs = refs[pos + 1:pos + 3]
    m_bufs = refs[pos + 3:pos + 5]
    kcb_ref, vcb_ref = refs[pos + 5:pos + 7]
    n_maps = 2 if diff else 1
    d = dq // n_maps
    past = kc_ref.shape[0]
    n_blocks = q_ref.shape[0] // tq

    kcb_ref[...] = kc_ref[...].astype(BF16)
    vcb_ref[...] = vc_ref[...].astype(BF16)

    def scores(qi, slot):
        q = q_ref[qi * tq:(qi + 1) * tq, :]
        for mp in range(n_maps):
            qm = q[:, mp * d:(mp + 1) * d]
            s_c = _qk(qm, kcb_ref[:, mp * d:(mp + 1) * d])
            s_l = _qk(qm, k_ref[:, mp * d:(mp + 1) * d])
            s_bufs[slot][mp, :, 0:past] = s_c
            s_bufs[slot][mp, :, past:] = s_l
            m_bufs[slot][mp] = jnp.maximum(jnp.max(s_c, axis=-1, keepdims=True),
                                           jnp.max(s_l, axis=-1, keepdims=True))

    def finish(qi, slot):
        rows = slice(qi * tq, (qi + 1) * tq)
        w, inv_l = _softmax_weights(s_bufs[slot], m_bufs[slot], lam)
        o = (jnp.dot(w[:, :past], vcb_ref[...], preferred_element_type=F32)
             + jnp.dot(w[:, past:], v_ref[...], preferred_element_type=F32)) * inv_l
        if diff:
            o = _rms(o, gsub_ref[...]) * (1.0 - lam_init)
        o_ref[rows, :] = (o * _silu(g_ref[rows, :].astype(F32))).astype(o_ref.dtype)

    scores(0, 0)
    for t in range(n_blocks):
        if t + 1 < n_blocks:
            scores(t + 1, (t + 1) % 2)
        finish(t, t % 2)


def _attention_latent(q, kc, kc_spec, vc, vc_spec, k, v, gate, n_heads, dq, dv, tq=256, diff=False,
                      lam_params=None, g_sub=None, lam_init=0.0):
    b, s, _ = q.shape
    past = kc_spec.block_shape[-2]
    n_maps = 2 if diff else 1
    head = lambda w: pl.BlockSpec((None, s, w), lambda bi, h: (bi, 0, h))
    in_specs = [head(dq), kc_spec, vc_spec, head(dq), head(dv), head(dv)]
    args = [q, kc, vc, k, v, gate]
    if diff:
        in_specs += [pl.BlockSpec(lam_params.shape, lambda bi, h: (0, 0)),
                     pl.BlockSpec((1, dv), lambda bi, h: (0, 0))]
        args += [lam_params, g_sub.reshape(1, dv)]
    return pl.pallas_call(
        functools.partial(_attn_latent_kernel, diff=diff, lam_init=lam_init, tq=tq,
                          dq=dq, dv=dv),
        out_shape=jax.ShapeDtypeStruct((b, s, n_heads * dv), BF16),
        grid=(b, n_heads),
        in_specs=in_specs,
        out_specs=head(dv),
        scratch_shapes=[pltpu.VMEM((n_maps, tq, past + s), F32), pltpu.VMEM((n_maps, tq, past + s), F32),
                        pltpu.VMEM((n_maps, tq, 1), F32), pltpu.VMEM((n_maps, tq, 1), F32),
                        pltpu.VMEM((past, dq), BF16), pltpu.VMEM((past, dv), BF16)],
        compiler_params=_params(("arbitrary", "arbitrary")),
        name="attn_latent_diff" if diff else "attn_latent",
    )(*args)


NA_Q_ROWS = 2
NA_K_ROWS = WIN_R + NA_Q_ROWS
NA_K_PAIRS = NA_K_ROWS * GRID_W // LANES
RPB_W = 2 * WIN_C - 1


def _na_blocks(rows):
    assert 2 * GRID_W == LANES and rows >= NA_K_ROWS
    wr = min(WIN_R, rows)
    r = np.arange(rows)
    rs = np.clip(r - wr // 2, 0, rows - wr)
    starts, variants, keys = [], [], {}
    for t in range(rows // NA_Q_ROWS):
        r0 = t * NA_Q_ROWS
        ws = int(np.clip(r0 - wr // 2, 0, rows - NA_K_ROWS))
        sig = (r0 - ws,) + tuple(int(rs[r0 + dq] - ws) for dq in range(NA_Q_ROWS))
        assert all(0 <= f and f + wr <= NA_K_ROWS for f in sig[1:])
        variants.append(keys.setdefault(sig, len(keys)))
        starts.append(ws)
    sigs = sorted(keys, key=keys.get)
    idx = np.zeros((len(sigs), NA_Q_ROWS, NA_K_PAIRS, LANES), np.int32)
    kind = np.ones(idx.shape, np.int32)
    for vi, sig in enumerate(sigs):
        for dq in range(NA_Q_ROWS):
            for j in range(NA_K_ROWS):
                lanes = slice((j % 2) * GRID_W, (j % 2) * GRID_W + RPB_W)
                if sig[1 + dq] <= j < sig[1 + dq] + wr:
                    dr = j - (sig[0] + dq)
                    idx[vi, dq, j // 2, lanes] = (dr + WIN_R - 1) * RPB_W + np.arange(RPB_W)
                    kind[vi, dq, j // 2, lanes] = 0
                else:
                    kind[vi, dq, j // 2, (j % 2) * GRID_W:(j % 2 + 1) * GRID_W] = 2
    return starts, variants, idx, kind


def _na_kernel(q_ref, k_ref, v_ref, kc_ref, vc_ref, rows_ref, g_ref, o_ref, bias_ref, *, starts, variants):
    nq = NA_Q_ROWS * GRID_W
    nk = NA_K_ROWS * GRID_W

    @pl.when(pl.program_id(1) == 0)
    def _():
        ci = lax.broadcasted_iota(jnp.int32, (GRID_W, LANES), 0)
        kc = lax.broadcasted_iota(jnp.int32, (GRID_W, LANES), 1) & (GRID_W - 1)
        cs = jnp.clip(ci - WIN_C // 2, 0, GRID_W - WIN_C)
        col_mask = jnp.where((kc >= cs) & (kc < cs + WIN_C), 0.0, NEG_BIG)
        for vi in range(bias_ref.shape[0]):
            for dq in range(NA_Q_ROWS):
                vecs = rows_ref[vi, dq]
                for jp in range(NA_K_PAIRS):
                    w = jnp.broadcast_to(vecs[jp:jp + 1, :], (GRID_W, LANES))
                    tile = pltpu.roll(w, LANES - (WIN_C - 1), 1, stride=1, stride_axis=0)
                    bias_ref[vi, dq * GRID_W:(dq + 1) * GRID_W, jp * LANES:(jp + 1) * LANES] = (
                        (tile + col_mask) * LOG2E)

    kc = kc_ref[...].astype(BF16)
    vc = vc_ref[...].astype(BF16)

    def body(t, carry):
        start = jnp.int32(starts[0])
        var = jnp.int32(variants[0])
        for ti in range(1, len(starts)):
            start = jnp.where(t == ti, jnp.int32(starts[ti]), start)
            var = jnp.where(t == ti, jnp.int32(variants[ti]), var)
        q0 = pl.multiple_of(t * nq, nq)
        k0 = pl.multiple_of(start * GRID_W, GRID_W)
        q = q_ref[pl.ds(q0, nq), :]
        kw = k_ref[pl.ds(k0, nk), :]
        vw = v_ref[pl.ds(k0, nk), :]
        s_w = _qk(q, kw) + bias_ref[var]
        s_c = _qk(q, kc)
        (p_w, p_c), l = _softmax_parts([s_w, s_c])
        o = (jnp.dot(p_w.astype(BF16), vw, preferred_element_type=F32)
             + jnp.dot(p_c.astype(BF16), vc, preferred_element_type=F32)) / l
        g = g_ref[pl.ds(q0, nq), :].astype(F32)
        o_ref[pl.ds(q0, nq), :] = (o * _silu(g)).astype(o_ref.dtype)
        return carry

    lax.fori_loop(0, len(starts), body, 0, unroll=4)


def _na_latent(q, k, v, cache_k, cache_v, layer, bias_rows, gate, n_heads, d, starts, variants):
    b, s, _ = q.shape
    past = cache_k.shape[2]
    n_var = bias_rows.shape[1]
    head = pl.BlockSpec((None, s, d), lambda h, bi: (bi, 0, h))
    ctx = pl.BlockSpec((None, None, past, d), lambda h, bi: (bi, layer, 0, h))
    return pl.pallas_call(
        functools.partial(_na_kernel, starts=tuple(starts), variants=tuple(variants)),
        out_shape=jax.ShapeDtypeStruct((b, s, n_heads * d), BF16),
        grid=(n_heads, b),
        in_specs=[head, head, head, ctx, ctx,
                  pl.BlockSpec((None,) + bias_rows.shape[1:], lambda h, bi: (h, 0, 0, 0, 0)),
                  head],
        out_specs=head,
        scratch_shapes=[pltpu.VMEM((n_var, NA_Q_ROWS * GRID_W, NA_K_ROWS * GRID_W), F32)],
        compiler_params=_params(("arbitrary", "arbitrary")),
        name="na_latent",
    )(q, k, v, cache_k, cache_v, bias_rows, gate)


def _merge_kernel(*refs, n_in):
    o_refs = refs[:n_in]
    w_ref, x_ref, gate_ref, g_ref, out_ref = refs[n_in:]
    y = None
    row = 0
    for o_ref in o_refs:
        width = o_ref.shape[1]
        part = jnp.dot(o_ref[...], w_ref[row:row + width, :], preferred_element_type=F32)
        y = part if y is None else y + part
        row += width
    out_ref[...] = x_ref[...] + gate_ref[...] * _rms(y, g_ref[...])


def _merge(o_list, w_out, x, ada_l, mod_row, g_post, tm=512):
    t, d = x.shape
    in_specs = [pl.BlockSpec((tm, o.shape[1]), lambda i: (i, 0)) for o in o_list]
    in_specs += [
        pl.BlockSpec(w_out.shape, lambda i: (0, 0)),
        pl.BlockSpec((tm, d), lambda i: (i, 0)),
        pl.BlockSpec((None, 1, d), lambda i: (mod_row(i), 0, 2)),
        pl.BlockSpec((1, d), lambda i: (0, 0)),
    ]
    return pl.pallas_call(
        functools.partial(_merge_kernel, n_in=len(o_list)),
        out_shape=jax.ShapeDtypeStruct((t, d), F32),
        grid=(t // tm,),
        in_specs=in_specs,
        out_specs=pl.BlockSpec((tm, d), lambda i: (i, 0)),
        compiler_params=_params(("arbitrary",)),
        name="merge",
    )(*o_list, w_out, x, ada_l, g_post.reshape(1, d))


def _rope_tables(s, n):
    a = n // 2
    half = a // 2
    t = jnp.arange(s)
    inv = ROPE_BASE ** (-jnp.arange(half, dtype=F32) / half)
    ang_r = (t // GRID_W).astype(F32)[:, None] * inv[None, :]
    ang_c = (t % GRID_W).astype(F32)[:, None] * inv[None, :]
    cos = jnp.concatenate([jnp.cos(ang_r)] * 2 + [jnp.cos(ang_c)] * 2, axis=-1)
    sin = jnp.concatenate([jnp.sin(ang_r)] * 2 + [jnp.sin(ang_c)] * 2, axis=-1)
    low = (np.arange(n) % a) < half
    sin_lo = jnp.where(low[None, :], -sin, 0.0)
    sin_hi = jnp.where(low[None, :], 0.0, sin)
    pad = LANES - n
    if pad:
        cos = jnp.pad(cos, ((0, 0), (0, pad)), constant_values=1.0)
        sin_lo = jnp.pad(sin_lo, ((0, 0), (0, pad)))
        sin_hi = jnp.pad(sin_hi, ((0, 0), (0, pad)))
    return (cos, sin_lo, sin_hi), half


def kernel(x_prompt, x_sample, cache_na_k, cache_na_v, cache_mla_ckv, cache_mla_kpe, cache_diff_k, cache_diff_v, c, c_ctx, w_ada, b_ada, g_pre, g_post, w_in_even, w_out_even, na_rpb, mla_g_q, mla_w_uq, mla_g_kv, mla_w_ukv, w_in_odd, w_out_odd, diff_lambda, diff_g):
    bp, sp, d = x_prompt.shape
    bs, ss, _ = x_sample.shape
    depth = w_ada.shape[0]
    n_even, n_odd = w_in_even.shape[0], w_in_odd.shape[0]
    past = cache_na_k.shape[2]
    na_heads, na_dim = cache_na_k.shape[3:]
    na_w = na_heads * na_dim
    kv_lora = cache_mla_ckv.shape[3]
    rope_w = cache_mla_kpe.shape[3]
    q_lora = mla_g_q.shape[1]
    diff_heads = cache_diff_k.shape[3]
    diff_w = diff_heads * cache_diff_k.shape[4]
    diff_d = diff_w // (2 * diff_heads)
    mla_w = w_out_even.shape[1] - na_w
    mla_v = mla_nope = LANES
    mla_heads = mla_w // mla_v
    na_scale = na_dim ** -0.5
    mla_scale = (mla_nope + rope_w) ** -0.5
    diff_scale = diff_d ** -0.5
    tp, ts = bp * sp, bs * ss
    tm = 512
    even_scales = (na_scale * LOG2E, 1.0, 1.0, 1.0, 1.0, 1.0, 1.0)
    odd_scales = (diff_scale * LOG2E, 1.0, 1.0, 1.0)

    ctx_row = bs
    cond = jnp.concatenate([c, c_ctx[None, :], jnp.zeros((16 - bs - 1, d), F32)], axis=0)
    ada = _ada(cond, w_ada, b_ada)
    row_p = lambda i: ctx_row
    row_s = lambda i: (i * tm) // ss

    rope_diff, half_diff = _rope_tables(ss, diff_d)
    rope_mla, half_mla = _rope_tables(ss, rope_w)

    starts, variants, rows_idx, rows_kind = _na_blocks(ss // GRID_W)

    xp = x_prompt.reshape(tp, d)
    xs = x_sample.reshape(ts, d)
    ck_na = cache_na_k.reshape(bs, -1, past, na_w)
    cv_na = cache_na_v.reshape(bs, -1, past, na_w)
    ck_diff = cache_diff_k.reshape(bs, -1, past, diff_w)
    cv_diff = cache_diff_v.reshape(bs, -1, past, diff_w)
    cache_kpe_pad = jnp.pad(cache_mla_kpe, ((0, 0), (0, 0), (0, 0), (0, LANES - rope_w)))

    new_na_k = new_na_v = new_ckv = new_kpe = new_dk = new_dv = None
    for l in range(depth):
        i = l // 2
        ada_l = ada[l].reshape(16, 1, 3 * d)
        if l % 2 == 0:
            kv_end = 4 * na_w + q_lora + kv_lora + rope_w
            kv_pad = 512 - kv_lora - rope_w
            w_in = jnp.concatenate([w_in_even[i][:, :kv_end].astype(BF16), jnp.zeros((d, kv_pad), BF16),
                                    w_in_even[i][:, kv_end:].astype(BF16)], axis=1)
            weights, o = [], 0
            for width in (na_w, na_w, na_w, na_w, q_lora, 512, mla_w):
                weights.append((w_in, o, width))
                o += width
            w_uq = mla_w_uq[i].reshape(q_lora, mla_heads, mla_nope + rope_w)
            w_uq_cat = jnp.pad(w_uq, ((0, 0), (0, 0), (0, 2 * LANES - mla_nope - rope_w)))
            w_uq_cat = w_uq_cat.reshape(q_lora, mla_heads * 2 * LANES).astype(BF16)
            w_ukv = mla_w_ukv[i].reshape(kv_lora, mla_heads, mla_nope + mla_v)
            w_uk = w_ukv[:, :, :mla_nope].reshape(kv_lora, mla_heads * mla_nope).astype(BF16)
            w_uv = w_ukv[:, :, mla_nope:].reshape(kv_lora, mla_heads * mla_v).astype(BF16)
            w_out = w_out_even[i].astype(BF16)
            rpb_flat = na_rpb[i].reshape(na_heads, -1)
            bias_rows = jnp.where(rows_kind[None] == 0, rpb_flat[:, rows_idx],
                                  jnp.where(rows_kind[None] == 1, 0.0, NEG_BIG))

            qa, ka, va, ga, cq, ckvkpe, gb, new_na_k, new_na_v = _proj(
                xp, sp, g_pre[l], ada_l, row_p, weights, [BF16, BF16, BF16, BF16, F32, F32, BF16], steps=4,
                out_scales=even_scales, cache_slots={1: (i, n_even, new_na_k, na_heads),
                                                     2: (i, n_even, new_na_v, na_heads)})
            q_cat = _mla_q(cq, sp, mla_g_q[i], w_uq_cat, mla_scale * LOG2E)
            k_cat, v_mla, new_ckv, new_kpe = _mla_kv(
                ckvkpe, pl.BlockSpec((tm, kv_lora), lambda i_: (i_, 0)),
                ckvkpe, pl.BlockSpec((tm, LANES), lambda i_: (i_, kv_lora // LANES)),
                tp, sp, w_uk, w_uv, g_kv=mla_g_kv[i], cache=(i, n_even, new_ckv, new_kpe), kpe_width=rope_w)
            r3 = lambda a: a.reshape(bp, sp, -1)
            oa = _attention(r3(qa), _seg3(r3(ka), sp, na_w) + _seg3(r3(va), sp, na_w), r3(ga),
                            na_heads, na_dim, na_dim, tq=sp, hps=na_heads)
            ob = _attention(r3(q_cat), _seg3(r3(k_cat), sp, mla_heads * 2 * LANES)
                            + _seg3(r3(v_mla), sp, mla_w),
                            r3(gb), mla_heads, 2 * LANES, mla_v, tq=sp, hps=mla_heads)
            xp = _merge([oa.reshape(tp, -1), ob.reshape(tp, -1)], w_out, xp, ada_l, row_p, g_post[l])

            qa, ka, va, ga, cq, ckvkpe, gb = _proj(
                xs, ss, g_pre[l], ada_l, row_s, weights, [BF16, BF16, BF16, BF16, F32, F32, BF16], steps=4,
                out_scales=even_scales)
            q_cat = _mla_q(cq, ss, mla_g_q[i], w_uq_cat, mla_scale * LOG2E, rope_tables=rope_mla,
                           rope_half=half_mla)
            k_cat, v_mla = _mla_kv(
                ckvkpe, pl.BlockSpec((tm, kv_lora), lambda i_: (i_, 0)),
                ckvkpe, pl.BlockSpec((tm, LANES), lambda i_: (i_, kv_lora // LANES)),
                ts, ss, w_uk, w_uv, g_kv=mla_g_kv[i], rope_tables=rope_mla, rope_half=half_mla)
            tiles_past = past // tm
            kc_cat, vc_mla = _mla_kv(
                cache_mla_ckv, pl.BlockSpec((None, None, tm, kv_lora),
                                            lambda i_: (i_ // tiles_past, i, i_ % tiles_past, 0)),
                cache_kpe_pad, pl.BlockSpec((None, None, tm, LANES),
                                            lambda i_: (i_ // tiles_past, i, i_ % tiles_past, 0)),
                bs * past, past, w_uk, w_uv)
            r3 = lambda a: a.reshape(bs, ss, -1)
            oa = _na_latent(r3(qa), r3(ka), r3(va), ck_na, cv_na, i, bias_rows, r3(ga), na_heads, na_dim,
                            starts, variants)
            c3 = lambda a: a.reshape(bs, past, -1)
            ob = _attention_latent(
                r3(q_cat),
                c3(kc_cat), pl.BlockSpec((None, past, 2 * LANES), lambda bi, h: (bi, 0, h)),
                c3(vc_mla), pl.BlockSpec((None, past, mla_v), lambda bi, h: (bi, 0, h)),
                r3(k_cat), r3(v_mla), r3(gb), mla_heads, 2 * LANES, mla_v)
            xs = _merge([oa.reshape(ts, -1), ob.reshape(ts, -1)], w_out, xs, ada_l, row_s, g_post[l])
        else:
            lam_init = 0.8 - 0.6 * math.exp(-0.3 * l)
            w_in = w_in_odd[i].astype(BF16)
            weights = [(w_in, k * diff_w, diff_w) for k in range(4)]
            w_out = w_out_odd[i].astype(BF16)
            dh = 2 * diff_d

            q, new_dk, new_dv, g = _proj(
                xp, sp, g_pre[l], ada_l, row_p, weights, [BF16, F32, F32, BF16], steps=4,
                out_scales=odd_scales, cache_slots={1: (i, n_odd, new_dk, 0), 2: (i, n_odd, new_dv, 0)})
            r3 = lambda a: a.reshape(bp, sp, -1)
            o = _attention(r3(q), _seg4(new_dk, i, sp, diff_w) + _seg4(new_dv, i, sp, diff_w), r3(g),
                           diff_heads, dh, dh, tq=sp, hps=diff_heads, diff=True,
                           lam_params=diff_lambda[i], g_sub=diff_g[i], lam_init=lam_init)
            xp = _merge([o.reshape(tp, -1)], w_out, xp, ada_l, row_p, g_post[l])

            q, k, v, g = _proj(xs, ss, g_pre[l], ada_l, row_s, weights, [BF16, BF16, BF16, BF16], steps=4,
                               rope_flags=(True, True, False, False), rope_tables=rope_diff,
                               rope_half=half_diff, out_scales=odd_scales)
            r3 = lambda a: a.reshape(bs, ss, -1)
            ctx_spec = pl.BlockSpec((None, None, past, dh), lambda bi, h: (bi, i, 0, h))
            o = _attention_latent(
                r3(q), ck_diff, ctx_spec, cv_diff, ctx_spec, r3(k), r3(v), r3(g), diff_heads, dh, dh,
                diff=True, lam_params=diff_lambda[i], g_sub=diff_g[i], lam_init=lam_init)
            xs = _merge([o.reshape(ts, -1)], w_out, xs, ada_l, row_s, g_post[l])

    return (xp.reshape(bp, sp, d), xs.reshape(bs, ss, d),
            new_na_k.reshape(bp, n_even, sp, na_heads, na_dim),
            new_na_v.reshape(bp, n_even, sp, na_heads, na_dim),
            new_ckv, new_kpe,
            new_dk.reshape(bp, n_odd, sp, diff_heads, 2 * diff_d),
            new_dv.reshape(bp, n_odd, sp, diff_heads, 2 * diff_d))
```

```python
import functools
import math

import numpy as np
import jax
import jax.numpy as jnp
from jax import lax
from jax.experimental import pallas as pl
from jax.experimental.pallas import tpu as pltpu

GRID_W = 64
WIN_R = 8
WIN_C = 16
EPS = 1e-6
ROPE_BASE = 10000.0
LOG2E = 1.4426950408889634
NEG_BIG = -1e30

LANES = 128
VMEM_LIMIT = 56 * 1024 * 1024

F32 = jnp.float32
BF16 = jnp.bfloat16


def _params(sem, vmem=VMEM_LIMIT):
    return pltpu.CompilerParams(dimension_semantics=sem, vmem_limit_bytes=vmem)


def _rms(x, g):
    return x * lax.rsqrt(jnp.mean(x * x, axis=-1, keepdims=True) + EPS) * g


def _silu(x):
    return x * (1.0 / (1.0 + jnp.exp(-x)))


def _rope(x, cos, sin_lo, sin_hi, half):
    return (x * cos + pltpu.roll(x, LANES - half, 1) * sin_lo
            + pltpu.roll(x, half, 1) * sin_hi)


def _slot_spec(rows, seq, width, slot, col_of):
    return pl.BlockSpec((rows, None, seq, width),
                        lambda *g: (g[0], slot, 0, col_of(*g)))


def _ada_kernel(cond_ref, w_ref, b_ref, o_ref):
    a = _silu(cond_ref[...]).astype(BF16)
    o_ref[...] = jnp.dot(a, w_ref[...].astype(BF16), preferred_element_type=F32) + b_ref[...]


def _ada(cond, w_ada, b_ada):
    depth, d, n = w_ada.shape
    rows = cond.shape[0]
    tn = 1024
    return pl.pallas_call(
        _ada_kernel,
        out_shape=jax.ShapeDtypeStruct((depth, rows, n), F32),
        grid=(depth, n // tn),
        in_specs=[
            pl.BlockSpec((rows, d), lambda l, j: (0, 0)),
            pl.BlockSpec((None, d, tn), lambda l, j: (l, 0, j)),
            pl.BlockSpec((None, 1, tn), lambda l, j: (l, 0, j)),
        ],
        out_specs=pl.BlockSpec((None, rows, tn), lambda l, j: (l, 0, j)),
        compiler_params=_params(("arbitrary", "arbitrary")),
        name="ada",
    )(cond, w_ada, b_ada.reshape(depth, 1, n))


def _proj_kernel(*refs, n_groups, rope_flags, rope_half, out_scales, n_alias, cache_heads, steps):
    refs = refs[n_alias:]
    x0_ref, xn_ref, g_ref, sh0_ref, sc0_ref, shn_ref, scn_ref = refs[:7]
    pos = 7
    if any(rope_flags):
        cos_ref, slo_ref, shi_ref = refs[pos:pos + 3]
        pos += 3
    w_refs = refs[pos:pos + n_groups]
    o_refs = refs[pos + n_groups:pos + 2 * n_groups]
    pos += 2 * n_groups
    c_refs = dict(zip(sorted(cache_heads), refs[pos:pos + len(cache_heads)]))
    h_ref = refs[pos + len(cache_heads)]
    i, j = pl.program_id(0), pl.program_id(1)
    slot = i % 2

    def modulated(x, sh_ref_, sc_ref_):
        return (_rms(x, g_ref[...]) * (1.0 + sc_ref_[...]) + sh_ref_[...]).astype(BF16)

    @pl.when((i == 0) & (j == 0))
    def _():
        h_ref[0] = modulated(x0_ref[...], sh0_ref, sc0_ref)

    h = h_ref[slot]
    for gi in range(n_groups):
        acc = jnp.dot(h, w_refs[gi][...], preferred_element_type=F32)
        if rope_flags[gi]:
            cos, slo, shi = cos_ref[...], slo_ref[...], shi_ref[...]
            parts = [_rope(acc[:, s:s + LANES], cos, slo, shi, rope_half)
                     for s in range(0, acc.shape[1], LANES)]
            acc = jnp.concatenate(parts, axis=1)
        if out_scales[gi] != 1.0:
            acc = acc * out_scales[gi]
        o_refs[gi][...] = acc.astype(o_refs[gi].dtype).reshape(o_refs[gi].shape)
        if gi in c_refs:
            c_ref, heads = c_refs[gi], cache_heads[gi]
            seq = c_ref.shape[1] // heads
            hw = c_ref.shape[2]
            for hl in range(acc.shape[1] // hw):
                hd = j * (acc.shape[1] // hw) + hl
                for bi in range(c_ref.shape[0]):
                    c_ref[bi, pl.ds(hd, seq, stride=heads), :] = acc[bi * seq:(bi + 1) * seq,
                                                                       hl * hw:(hl + 1) * hw]

    rows = h_ref.shape[1] // steps
    r0 = pl.multiple_of(j * rows, rows)
    h_ref[1 - slot, pl.ds(r0, rows), :] = modulated(xn_ref[pl.ds(r0, rows), :], shn_ref, scn_ref)


def _proj(x, seq, g_pre, ada_l, mod_row, weights, out_dtypes, steps, rope_flags=None,
          rope_tables=None, rope_half=0, out_scales=None, cache_slots=None, tm=512):
    t, d = x.shape
    n_groups = len(weights)
    n_tiles = t // tm
    rope_flags = tuple(rope_flags) if rope_flags is not None else (False,) * n_groups
    out_scales = tuple(out_scales) if out_scales is not None else (1.0,) * n_groups
    cache_slots = cache_slots or {}
    tiles_per_seq = max(seq // tm, 1)
    nxt = lambda i: jnp.minimum(i + 1, n_tiles - 1)
    aliased = [(gi, cs[2]) for gi, cs in sorted(cache_slots.items()) if cs[2] is not None]
    in_specs = [pl.BlockSpec(memory_space=pl.ANY) for _ in aliased]
    in_specs += [
        pl.BlockSpec((tm, d), lambda i, j: (0, 0)),
        pl.BlockSpec((tm, d), lambda i, j: (nxt(i), 0)),
        pl.BlockSpec((1, d), lambda i, j: (0, 0)),
        pl.BlockSpec((None, 1, d), lambda i, j: (mod_row(0), 0, 0)),
        pl.BlockSpec((None, 1, d), lambda i, j: (mod_row(0), 0, 1)),
        pl.BlockSpec((None, 1, d), lambda i, j: (mod_row(nxt(i)), 0, 0)),
        pl.BlockSpec((None, 1, d), lambda i, j: (mod_row(nxt(i)), 0, 1)),
    ]
    args = [prev for _, prev in aliased] + [x, x, g_pre.reshape(1, d), ada_l, ada_l, ada_l, ada_l]
    if any(rope_flags):
        for tab in rope_tables:
            in_specs.append(pl.BlockSpec((tm, LANES), lambda i, j: (i % tiles_per_seq, 0)))
            args.append(tab)
    out_specs, out_shapes = [], []
    for gi, ((w, layer, col0, n), dt) in enumerate(zip(weights, out_dtypes)):
        tn = n // steps
        in_specs.append(pl.BlockSpec(
            (None, d, tn), functools.partial(lambda i, j, ly, b0: (ly, 0, b0 + j), ly=layer, b0=col0 // tn)))
        args.append(w)
        if gi in cache_slots and not cache_slots[gi][3]:
            slot, n_slots = cache_slots[gi][:2]
            out_specs.append(_slot_spec(tm // seq, seq, tn, slot, lambda i, j: j))
            out_shapes.append(jax.ShapeDtypeStruct((t // seq, n_slots, seq, n), dt))
        else:
            out_specs.append(pl.BlockSpec((tm, tn), lambda i, j: (i, j)))
            out_shapes.append(jax.ShapeDtypeStruct((t, n), dt))
    merged = [gi for gi in sorted(cache_slots) if cache_slots[gi][3]]
    for gi in merged:
        slot, n_slots, _, heads = cache_slots[gi]
        n = weights[gi][3]
        out_specs.append(_slot_spec(tm // seq, seq * heads, n // heads, slot, lambda i, j: 0))
        out_shapes.append(jax.ShapeDtypeStruct((t // seq, n_slots, seq * heads, n // heads), F32))
    cache_out = lambda gi: n_groups + merged.index(gi) if gi in merged else gi
    return pl.pallas_call(
        functools.partial(_proj_kernel, n_groups=n_groups, rope_flags=rope_flags, rope_half=rope_half,
                          out_scales=out_scales, n_alias=len(aliased),
                          cache_heads={gi: cache_slots[gi][3] for gi in merged}, steps=steps),
        out_shape=out_shapes,
        grid=(n_tiles, steps),
        in_specs=in_specs,
        out_specs=out_specs,
        input_output_aliases={k: cache_out(gi) for k, (gi, _) in enumerate(aliased)},
        scratch_shapes=[pltpu.VMEM((2, tm, d), BF16)],
        compiler_params=_params(("arbitrary", "arbitrary")),
        name="proj",
    )(*args)


def _mla_q_kernel(*refs, rope, rope_half, q_scale):
    if rope:
        cq_ref, g_ref, w_ref, cos_ref, slo_ref, shi_ref, o_ref = refs
    else:
        cq_ref, g_ref, w_ref, o_ref = refs
    n = _rms(cq_ref[...], g_ref[...]).astype(BF16)
    q = jnp.dot(n, w_ref[...], preferred_element_type=F32) * q_scale
    if rope:
        cos, slo, shi = cos_ref[...], slo_ref[...], shi_ref[...]
        parts = []
        for s in range(0, q.shape[1], 2 * LANES):
            parts.append(q[:, s:s + LANES])
            parts.append(_rope(q[:, s + LANES:s + 2 * LANES], cos, slo, shi, rope_half))
        q = jnp.concatenate(parts, axis=1)
    o_ref[...] = q.astype(o_ref.dtype)


def _mla_q(cq, seq, g_q, w_uq_cat, q_scale, rope_tables=None, rope_half=0, tm=512):
    t, r = cq.shape
    n = w_uq_cat.shape[1]
    rope = rope_tables is not None
    tiles_per_seq = max(seq // tm, 1)
    in_specs = [
        pl.BlockSpec((tm, r), lambda i: (i, 0)),
        pl.BlockSpec((1, r), lambda i: (0, 0)),
        pl.BlockSpec((r, n), lambda i: (0, 0)),
    ]
    args = [cq, g_q.reshape(1, r), w_uq_cat]
    if rope:
        for tab in rope_tables:
            in_specs.append(pl.BlockSpec((tm, LANES), lambda i: (i % tiles_per_seq, 0)))
            args.append(tab)
    return pl.pallas_call(
        functools.partial(_mla_q_kernel, rope=rope, rope_half=rope_half, q_scale=q_scale),
        out_shape=jax.ShapeDtypeStruct((t, n), BF16),
        grid=(t // tm,),
        in_specs=in_specs,
        out_specs=pl.BlockSpec((tm, n), lambda i: (i, 0)),
        compiler_params=_params(("arbitrary",)),
        name="mla_q",
    )(*args)


def _mla_kv_kernel(*refs, normalise, rope, rope_half, emit_cache, n_alias, n_heads, nope):
    refs = list(refs[n_alias:])
    ckv_ref, kpe_ref = refs[:2]
    pos = 2
    if normalise:
        g_ref = refs[pos]
        pos += 1
    wk_ref, wv_ref = refs[pos:pos + 2]
    pos += 2
    if rope:
        cos_ref, slo_ref, shi_ref = refs[pos:pos + 3]
        pos += 3
    kcat_ref, v_ref = refs[pos:pos + 2]
    pos += 2
    ckv = ckv_ref[...]
    if normalise:
        ckv = _rms(ckv, g_ref[...])
    kpe = kpe_ref[...]
    if emit_cache:
        ckv_out_ref, kpe_out_ref = refs[pos:pos + 2]
        ckv_out_ref[...] = ckv.reshape(ckv_out_ref.shape)
        kpe_out_ref[...] = kpe[:, :kpe_out_ref.shape[-1]].reshape(kpe_out_ref.shape)
    if rope:
        kpe = _rope(kpe, cos_ref[...], slo_ref[...], shi_ref[...], rope_half)
    cb = ckv.astype(BF16)
    kn = jnp.dot(cb, wk_ref[...], preferred_element_type=F32).astype(BF16)
    v_ref[...] = jnp.dot(cb, wv_ref[...], preferred_element_type=F32).astype(BF16)
    kpe_b = kpe.astype(BF16)
    for h in range(n_heads):
        kcat_ref[:, h * 2 * nope:h * 2 * nope + nope] = kn[:, h * nope:(h + 1) * nope]
        kcat_ref[:, h * 2 * nope + nope:(h + 1) * 2 * nope] = kpe_b


def _mla_kv(ckv_arr, ckv_spec, kpe_arr, kpe_spec, t, seq, w_uk, w_uv, g_kv=None, rope_tables=None,
            rope_half=0, cache=None, kpe_width=64, tm=512):
    lora, nk = w_uk.shape
    nv = w_uv.shape[1]
    nope = LANES
    n_heads = nk // nope
    normalise = g_kv is not None
    rope = rope_tables is not None
    tiles_per_seq = max(seq // tm, 1)
    aliased = [] if cache is None or cache[2] is None else [cache[2], cache[3]]
    in_specs = [pl.BlockSpec(memory_space=pl.ANY) for _ in aliased] + [ckv_spec, kpe_spec]
    args = aliased + [ckv_arr, kpe_arr]
    if normalise:
        in_specs.append(pl.BlockSpec((1, lora), lambda i: (0, 0)))
        args.append(g_kv.reshape(1, lora))
    in_specs += [pl.BlockSpec((lora, nk), lambda i: (0, 0)), pl.BlockSpec((lora, nv), lambda i: (0, 0))]
    args += [w_uk, w_uv]
    if rope:
        for tab in rope_tables:
            in_specs.append(pl.BlockSpec((tm, LANES), lambda i: (i % tiles_per_seq, 0)))
            args.append(tab)
    out_shapes = [jax.ShapeDtypeStruct((t, 2 * nk), BF16), jax.ShapeDtypeStruct((t, nv), BF16)]
    out_specs = [pl.BlockSpec((tm, 2 * nk), lambda i: (i, 0)), pl.BlockSpec((tm, nv), lambda i: (i, 0))]
    if cache is not None:
        slot, n_slots = cache[:2]
        out_shapes += [jax.ShapeDtypeStruct((t // seq, n_slots, seq, lora), F32),
                       jax.ShapeDtypeStruct((t // seq, n_slots, seq, kpe_width), F32)]
        out_specs += [_slot_spec(tm // seq, seq, lora, slot, lambda i: 0),
                      _slot_spec(tm // seq, seq, kpe_width, slot, lambda i: 0)]
    return pl.pallas_call(
        functools.partial(_mla_kv_kernel, normalise=normalise, rope=rope, rope_half=rope_half,
                          emit_cache=cache is not None, n_alias=len(aliased), n_heads=n_heads, nope=nope),
        out_shape=out_shapes,
        grid=(t // tm,),
        in_specs=in_specs,
        out_specs=out_specs,
        input_output_aliases={k: 2 + k for k in range(len(aliased))},
        compiler_params=_params(("arbitrary",)),
        name="mla_kv",
    )(*args)


def _softmax_parts(s_list):
    m = functools.reduce(jnp.maximum, [jnp.max(s, axis=-1, keepdims=True) for s in s_list])
    p_list = [jnp.exp2(s - m) for s in s_list]
    l = functools.reduce(lambda a, b: a + b, [jnp.sum(p, axis=-1, keepdims=True) for p in p_list])
    return p_list, l


def _qk(q, k):
    return lax.dot_general(q, k, (((1,), (1,)), ((), ())), preferred_element_type=F32)


def _diff_lambda(lam_ref, lam_init):
    lp = lam_ref[...]
    return (jnp.exp(jnp.sum(lp[0:1] * lp[1:2], axis=-1, keepdims=True))
            - jnp.exp(jnp.sum(lp[2:3] * lp[3:4], axis=-1, keepdims=True)) + lam_init)


def _softmax_weights(s_ref, m_ref, lam):
    ps = [jnp.exp2(s_ref[mp] - m_ref[mp]) for mp in range(s_ref.shape[0])]
    ls = [jnp.sum(p, axis=-1, keepdims=True) for p in ps]
    w = ps[0].astype(BF16)
    if len(ps) == 2:
        w = w - (lam * ls[0] / ls[1]).astype(BF16) * ps[1].astype(BF16)
    return w, 1.0 / ls[0]


def _attn_kernel(*refs, diff, lam_init, hps, dq, dv):
    q_ref, k_ref, v_ref, g_ref = refs[:4]
    pos = 4
    if diff:
        lam_ref, gsub_ref = refs[pos:pos + 2]
        pos += 2
        lam = _diff_lambda(lam_ref, lam_init)
    o_ref = refs[pos]
    for hh in range(hps):
        q = q_ref[:, hh * dq:(hh + 1) * dq].astype(BF16)
        k = k_ref[:, hh * dq:(hh + 1) * dq].astype(BF16)
        v = v_ref[:, hh * dv:(hh + 1) * dv].astype(BF16)
        if not diff:
            (p,), l = _softmax_parts([_qk(q, k)])
            o = jnp.dot(p.astype(BF16), v, preferred_element_type=F32) / l
        else:
            d = dq // 2
            (p1,), l1 = _softmax_parts([_qk(q[:, :d], k[:, :d])])
            (p2,), l2 = _softmax_parts([_qk(q[:, d:], k[:, d:])])
            w = p1 * (1.0 / l1) - p2 * (lam / l2)
            o = jnp.dot(w.astype(BF16), v, preferred_element_type=F32)
            o = _rms(o, gsub_ref[...]) * (1.0 - lam_init)
        g = g_ref[:, hh * dv:(hh + 1) * dv].astype(F32)
        o_ref[:, hh * dv:(hh + 1) * dv] = (o * _silu(g)).astype(o_ref.dtype)


def _attention(q, seg, gate, n_heads, dq, dv, tq, hps=1, diff=False, lam_params=None,
               g_sub=None, lam_init=0.0):
    b, sq, _ = q.shape
    k_arr, k_spec, v_arr, v_spec = seg
    in_specs = [pl.BlockSpec((None, tq, hps * dq), lambda bi, h, qi: (bi, qi, h)), k_spec, v_spec,
                pl.BlockSpec((None, tq, hps * dv), lambda bi, h, qi: (bi, qi, h))]
    args = [q, k_arr, v_arr, gate]
    if diff:
        in_specs += [pl.BlockSpec(lam_params.shape, lambda bi, h, qi: (0, 0)),
                     pl.BlockSpec((1, dv), lambda bi, h, qi: (0, 0))]
        args += [lam_params, g_sub.reshape(1, dv)]
    return pl.pallas_call(
        functools.partial(_attn_kernel, diff=diff, lam_init=lam_init, hps=hps, dq=dq, dv=dv),
        out_shape=jax.ShapeDtypeStruct((b, sq, n_heads * dv), BF16),
        grid=(b, n_heads // hps, sq // tq),
        in_specs=in_specs,
        out_specs=pl.BlockSpec((None, tq, hps * dv), lambda bi, h, qi: (bi, qi, h)),
        compiler_params=_params(("arbitrary", "arbitrary", "arbitrary")),
        name="attn_diff" if diff else "attn",
    )(*args)


def _seg3(arr, sk, width):
    return arr, pl.BlockSpec((None, sk, width), lambda bi, h, qi: (bi, 0, h))


def _seg4(arr, layer, sk, width):
    return arr, pl.BlockSpec((None, None, sk, width), lambda bi, h, qi: (bi, layer, 0, h))


def _attn_latent_kernel(*refs, diff, lam_init, tq, dq, dv):
    q_ref, kc_ref, vc_ref, k_ref, v_ref, g_ref = refs[:6]
    pos = 6
    lam = None
    if diff:
        lam_ref, gsub_ref = refs[pos:pos + 2]
        pos += 2
        lam = _diff_lambda(lam_ref, lam_init)
    o_ref = refs[pos]
    s_bufs = refs[pos + 1:pos + 3]
    m_bufs = refs[pos + 3:pos + 5]
    kcb_ref, vcb_ref = refs[pos + 5:pos + 7]
    n_maps = 2 if diff else 1
    d = dq // n_maps
    past = kc_ref.shape[0]
    n_blocks = q_ref.shape[0] // tq

    kcb_ref[...] = kc_ref[...].astype(BF16)
    vcb_ref[...] = vc_ref[...].astype(BF16)

    def scores(qi, slot):
        q = q_ref[qi * tq:(qi + 1) * tq, :]
        for mp in range(n_maps):
            qm = q[:, mp * d:(mp + 1) * d]
            s_c = _qk(qm, kcb_ref[:, mp * d:(mp + 1) * d])
            s_l = _qk(qm, k_ref[:, mp * d:(mp + 1) * d])
            s_bufs[slot][mp, :, 0:past] = s_c
            s_bufs[slot][mp, :, past:] = s_l
            m_bufs[slot][mp] = jnp.maximum(jnp.max(s_c, axis=-1, keepdims=True),
                                           jnp.max(s_l, axis=-1, keepdims=True))

    def finish(qi, slot):
        rows = slice(qi * tq, (qi + 1) * tq)
        w, inv_l = _softmax_weights(s_bufs[slot], m_bufs[slot], lam)
        o = (jnp.dot(w[:, :past], vcb_ref[...], preferred_element_type=F32)
             + jnp.dot(w[:, past:], v_ref[...], preferred_element_type=F32)) * inv_l
        if diff:
            o = _rms(o, gsub_ref[...]) * (1.0 - lam_init)
        o_ref[rows, :] = (o * _silu(g_ref[rows, :].astype(F32))).astype(o_ref.dtype)

    scores(0, 0)
    for t in range(n_blocks):
        if t + 1 < n_blocks:
            scores(t + 1, (t + 1) % 2)
        finish(t, t % 2)


def _attention_latent(q, kc, kc_spec, vc, vc_spec, k, v, gate, n_heads, dq, dv, tq=256, diff=False,
                      lam_params=None, g_sub=None, lam_init=0.0):
    b, s, _ = q.shape
    past = kc_spec.block_shape[-2]
    n_maps = 2 if diff else 1
    head = lambda w: pl.BlockSpec((None, s, w), lambda bi, h: (bi, 0, h))
    in_specs = [head(dq), kc_spec, vc_spec, head(dq), head(dv), head(dv)]
    args = [q, kc, vc, k, v, gate]
    if diff:
        in_specs += [pl.BlockSpec(lam_params.shape, lambda bi, h: (0, 0)),
                     pl.BlockSpec((1, dv), lambda bi, h: (0, 0))]
        args += [lam_params, g_sub.reshape(1, dv)]
    return pl.pallas_call(
        functools.partial(_attn_latent_kernel, diff=diff, lam_init=lam_init, tq=tq,
                          dq=dq, dv=dv),
        out_shape=jax.ShapeDtypeStruct((b, s, n_heads * dv), BF16),
        grid=(b, n_heads),
        in_specs=in_specs,
        out_specs=head(dv),
        scratch_shapes=[pltpu.VMEM((n_maps, tq, past + s), F32), pltpu.VMEM((n_maps, tq, past + s), F32),
                        pltpu.VMEM((n_maps, tq, 1), F32), pltpu.VMEM((n_maps, tq, 1), F32),
                        pltpu.VMEM((past, dq), BF16), pltpu.VMEM((past, dv), BF16)],
        compiler_params=_params(("arbitrary", "arbitrary")),
        name="attn_latent_diff" if diff else "attn_latent",
    )(*args)


NA_Q_ROWS = 4
NA_K_ROWS = WIN_R + NA_Q_ROWS
NA_K_PAIRS = NA_K_ROWS * GRID_W // LANES
RPB_W = 2 * WIN_C - 1


def _na_blocks(rows):
    assert 2 * GRID_W == LANES and rows >= NA_K_ROWS
    wr = min(WIN_R, rows)
    r = np.arange(rows)
    rs = np.clip(r - wr // 2, 0, rows - wr)
    starts, variants, keys = [], [], {}
    for t in range(rows // NA_Q_ROWS):
        r0 = t * NA_Q_ROWS
        ws = int(np.clip(r0 - wr // 2, 0, rows - NA_K_ROWS))
        sig = (r0 - ws,) + tuple(int(rs[r0 + dq] - ws) for dq in range(NA_Q_ROWS))
        assert all(0 <= f and f + wr <= NA_K_ROWS for f in sig[1:])
        variants.append(keys.setdefault(sig, len(keys)))
        starts.append(ws)
    sigs = sorted(keys, key=keys.get)
    idx = np.zeros((len(sigs), NA_Q_ROWS, NA_K_PAIRS, LANES), np.int32)
    kind = np.ones(idx.shape, np.int32)
    for vi, sig in enumerate(sigs):
        for dq in range(NA_Q_ROWS):
            for j in range(NA_K_ROWS):
                lanes = slice((j % 2) * GRID_W, (j % 2) * GRID_W + RPB_W)
                if sig[1 + dq] <= j < sig[1 + dq] + wr:
                    dr = j - (sig[0] + dq)
                    idx[vi, dq, j // 2, lanes] = (dr + WIN_R - 1) * RPB_W + np.arange(RPB_W)
                    kind[vi, dq, j // 2, lanes] = 0
                else:
                    kind[vi, dq, j // 2, (j % 2) * GRID_W:(j % 2 + 1) * GRID_W] = 2
    return starts, variants, idx, kind


def _na_kernel(q_ref, k_ref, v_ref, kc_ref, vc_ref, rows_ref, g_ref, o_ref, bias_ref, *, starts, variants):
    nq = NA_Q_ROWS * GRID_W
    nk = NA_K_ROWS * GRID_W

    @pl.when(pl.program_id(1) == 0)
    def _():
        ci = lax.broadcasted_iota(jnp.int32, (GRID_W, LANES), 0)
        kc = lax.broadcasted_iota(jnp.int32, (GRID_W, LANES), 1) & (GRID_W - 1)
        cs = jnp.clip(ci - WIN_C // 2, 0, GRID_W - WIN_C)
        col_mask = jnp.where((kc >= cs) & (kc < cs + WIN_C), 0.0, NEG_BIG)
        for vi in range(bias_ref.shape[0]):
            for dq in range(NA_Q_ROWS):
                vecs = rows_ref[vi, dq]
                for jp in range(NA_K_PAIRS):
                    w = jnp.broadcast_to(vecs[jp:jp + 1, :], (GRID_W, LANES))
                    tile = pltpu.roll(w, LANES - (WIN_C - 1), 1, stride=1, stride_axis=0)
                    bias_ref[vi, dq * GRID_W:(dq + 1) * GRID_W, jp * LANES:(jp + 1) * LANES] = (
                        (tile + col_mask) * LOG2E)

    kc = kc_ref[...].astype(BF16)
    vc = vc_ref[...].astype(BF16)

    def body(t, carry):
        start = jnp.int32(starts[0])
        var = jnp.int32(variants[0])
        for ti in range(1, len(starts)):
            start = jnp.where(t == ti, jnp.int32(starts[ti]), start)
            var = jnp.where(t == ti, jnp.int32(variants[ti]), var)
        q0 = pl.multiple_of(t * nq, nq)
        k0 = pl.multiple_of(start * GRID_W, GRID_W)
        q = q_ref[pl.ds(q0, nq), :]
        kw = k_ref[pl.ds(k0, nk), :]
        vw = v_ref[pl.ds(k0, nk), :]
        s_w = _qk(q, kw) + bias_ref[var]
        s_c = _qk(q, kc)
        (p_w, p_c), l = _softmax_parts([s_w, s_c])
        o = (jnp.dot(p_w.astype(BF16), vw, preferred_element_type=F32)
             + jnp.dot(p_c.astype(BF16), vc, preferred_element_type=F32)) / l
        g = g_ref[pl.ds(q0, nq), :].astype(F32)
        o_ref[pl.ds(q0, nq), :] = (o * _silu(g)).astype(o_ref.dtype)
        return carry

    lax.fori_loop(0, len(starts), body, 0, unroll=8)


def _na_latent(q, k, v, cache_k, cache_v, layer, bias_rows, gate, n_heads, d, starts, variants):
    b, s, _ = q.shape
    past = cache_k.shape[2]
    n_var = bias_rows.shape[1]
    head = pl.BlockSpec((None, s, d), lambda h, bi: (bi, 0, h))
    ctx = pl.BlockSpec((None, None, past, d), lambda h, bi: (bi, layer, 0, h))
    return pl.pallas_call(
        functools.partial(_na_kernel, starts=tuple(starts), variants=tuple(variants)),
        out_shape=jax.ShapeDtypeStruct((b, s, n_heads * d), BF16),
        grid=(n_heads, b),
        in_specs=[head, head, head, ctx, ctx,
                  pl.BlockSpec((None,) + bias_rows.shape[1:], lambda h, bi: (h, 0, 0, 0, 0)),
                  head],
        out_specs=head,
        scratch_shapes=[pltpu.VMEM((n_var, NA_Q_ROWS * GRID_W, NA_K_ROWS * GRID_W), F32)],
        compiler_params=_params(("arbitrary", "arbitrary")),
        name="na_latent",
    )(q, k, v, cache_k, cache_v, bias_rows, gate)


def _merge_kernel(*refs, n_in):
    o_refs = refs[:n_in]
    w_ref, x_ref, gate_ref, g_ref, out_ref = refs[n_in:]
    y = None
    row = 0
    for o_ref in o_refs:
        width = o_ref.shape[1]
        part = jnp.dot(o_ref[...], w_ref[row:row + width, :], preferred_element_type=F32)
        y = part if y is None else y + part
        row += width
    out_ref[...] = x_ref[...] + gate_ref[...] * _rms(y, g_ref[...])


def _merge(o_list, w_out, layer, x, ada_l, mod_row, g_post, tm=512):
    t, d = x.shape
    in_specs = [pl.BlockSpec((tm, o.shape[1]), lambda i: (i, 0)) for o in o_list]
    in_specs += [
        pl.BlockSpec((None,) + w_out.shape[1:], lambda i: (layer, 0, 0)),
        pl.BlockSpec((tm, d), lambda i: (i, 0)),
        pl.BlockSpec((None, 1, d), lambda i: (mod_row(i), 0, 2)),
        pl.BlockSpec((1, d), lambda i: (0, 0)),
    ]
    return pl.pallas_call(
        functools.partial(_merge_kernel, n_in=len(o_list)),
        out_shape=jax.ShapeDtypeStruct((t, d), F32),
        grid=(t // tm,),
        in_specs=in_specs,
        out_specs=pl.BlockSpec((tm, d), lambda i: (i, 0)),
        compiler_params=_params(("arbitrary",)),
        name="merge",
    )(*o_list, w_out, x, ada_l, g_post.reshape(1, d))


def _rope_tables(s, n):
    a = n // 2
    half = a // 2
    t = jnp.arange(s)
    inv = ROPE_BASE ** (-jnp.arange(half, dtype=F32) / half)
    ang_r = (t // GRID_W).astype(F32)[:, None] * inv[None, :]
    ang_c = (t % GRID_W).astype(F32)[:, None] * inv[None, :]
    cos = jnp.concatenate([jnp.cos(ang_r)] * 2 + [jnp.cos(ang_c)] * 2, axis=-1)
    sin = jnp.concatenate([jnp.sin(ang_r)] * 2 + [jnp.sin(ang_c)] * 2, axis=-1)
    low = (np.arange(n) % a) < half
    sin_lo = jnp.where(low[None, :], -sin, 0.0)
    sin_hi = jnp.where(low[None, :], 0.0, sin)
    pad = LANES - n
    if pad:
        cos = jnp.pad(cos, ((0, 0), (0, pad)), constant_values=1.0)
        sin_lo = jnp.pad(sin_lo, ((0, 0), (0, pad)))
        sin_hi = jnp.pad(sin_hi, ((0, 0), (0, pad)))
    return (cos, sin_lo, sin_hi), half


def kernel(x_prompt, x_sample, cache_na_k, cache_na_v, cache_mla_ckv, cache_mla_kpe, cache_diff_k, cache_diff_v, c, c_ctx, w_ada, b_ada, g_pre, g_post, w_in_even, w_out_even, na_rpb, mla_g_q, mla_w_uq, mla_g_kv, mla_w_ukv, w_in_odd, w_out_odd, diff_lambda, diff_g):
    bp, sp, d = x_prompt.shape
    bs, ss, _ = x_sample.shape
    depth = w_ada.shape[0]
    n_even, n_odd = w_in_even.shape[0], w_in_odd.shape[0]
    past = cache_na_k.shape[2]
    na_heads, na_dim = cache_na_k.shape[3:]
    na_w = na_heads * na_dim
    kv_lora = cache_mla_ckv.shape[3]
    rope_w = cache_mla_kpe.shape[3]
    q_lora = mla_g_q.shape[1]
    diff_heads = cache_diff_k.shape[3]
    diff_w = diff_heads * cache_diff_k.shape[4]
    diff_d = diff_w // (2 * diff_heads)
    mla_w = w_out_even.shape[1] - na_w
    mla_v = mla_nope = LANES
    mla_heads = mla_w // mla_v
    na_scale = na_dim ** -0.5
    mla_scale = (mla_nope + rope_w) ** -0.5
    diff_scale = diff_d ** -0.5
    tp, ts = bp * sp, bs * ss
    tm = 512
    even_scales = (na_scale * LOG2E, 1.0, 1.0, 1.0, 1.0, 1.0, 1.0)
    odd_scales = (diff_scale * LOG2E, 1.0, 1.0, 1.0)

    ctx_row = bs
    cond = jnp.concatenate([c, c_ctx[None, :], jnp.zeros((16 - bs - 1, d), F32)], axis=0)
    ada = _ada(cond, w_ada, b_ada)
    row_p = lambda i: ctx_row
    row_s = lambda i: (i * tm) // ss

    rope_diff, half_diff = _rope_tables(ss, diff_d)
    rope_mla, half_mla = _rope_tables(ss, rope_w)

    starts, variants, rows_idx, rows_kind = _na_blocks(ss // GRID_W)

    xp = x_prompt.reshape(tp, d)
    xs = x_sample.reshape(ts, d)
    ck_na = cache_na_k.reshape(bs, -1, past, na_w)
    cv_na = cache_na_v.reshape(bs, -1, past, na_w)
    ck_diff = cache_diff_k.reshape(bs, -1, past, diff_w)
    cv_diff = cache_diff_v.reshape(bs, -1, past, diff_w)
    cache_kpe_pad = jnp.pad(cache_mla_kpe, ((0, 0), (0, 0), (0, 0), (0, LANES - rope_w)))

    kv_end = 4 * na_w + q_lora + kv_lora + rope_w
    kv_pad = 512 - kv_lora - rope_w
    w_even = jnp.concatenate([w_in_even[:, :, :kv_end], jnp.zeros((n_even, d, kv_pad), F32),
                              w_in_even[:, :, kv_end:]], axis=2).astype(BF16)
    w_odd = w_in_odd.astype(BF16)
    w_out_even_b = w_out_even.astype(BF16)
    w_out_odd_b = w_out_odd.astype(BF16)

    new_na_k = new_na_v = new_ckv = new_kpe = new_dk = new_dv = None
    for l in range(depth):
        i = l // 2
        ada_l = ada[l].reshape(16, 1, 3 * d)
        if l % 2 == 0:
            weights, o = [], 0
            for width in (na_w, na_w, na_w, na_w, q_lora, 512, mla_w):
                weights.append((w_even, i, o, width))
                o += width
            w_uq = mla_w_uq[i].reshape(q_lora, mla_heads, mla_nope + rope_w)
            w_uq_cat = jnp.pad(w_uq, ((0, 0), (0, 0), (0, 2 * LANES - mla_nope - rope_w)))
            w_uq_cat = w_uq_cat.reshape(q_lora, mla_heads * 2 * LANES).astype(BF16)
            w_ukv = mla_w_ukv[i].reshape(kv_lora, mla_heads, mla_nope + mla_v)
            w_uk = w_ukv[:, :, :mla_nope].reshape(kv_lora, mla_heads * mla_nope).astype(BF16)
            w_uv = w_ukv[:, :, mla_nope:].reshape(kv_lora, mla_heads * mla_v).astype(BF16)
            rpb_flat = na_rpb[i].reshape(na_heads, -1)
            bias_rows = jnp.where(rows_kind[None] == 0, rpb_flat[:, rows_idx],
                                  jnp.where(rows_kind[None] == 1, 0.0, NEG_BIG))

            qa, ka, va, ga, cq, ckvkpe, gb, new_na_k, new_na_v = _proj(
                xp, sp, g_pre[l], ada_l, row_p, weights, [BF16, BF16, BF16, BF16, F32, F32, BF16], steps=4,
                out_scales=even_scales, cache_slots={1: (i, n_even, new_na_k, na_heads),
                                                     2: (i, n_even, new_na_v, na_heads)})
            q_cat = _mla_q(cq, sp, mla_g_q[i], w_uq_cat, mla_scale * LOG2E)
            k_cat, v_mla, new_ckv, new_kpe = _mla_kv(
                ckvkpe, pl.BlockSpec((tm, kv_lora), lambda i_: (i_, 0)),
                ckvkpe, pl.BlockSpec((tm, LANES), lambda i_: (i_, kv_lora // LANES)),
                tp, sp, w_uk, w_uv, g_kv=mla_g_kv[i], cache=(i, n_even, new_ckv, new_kpe), kpe_width=rope_w)
            r3 = lambda a: a.reshape(bp, sp, -1)
            oa = _attention(r3(qa), _seg3(r3(ka), sp, na_w) + _seg3(r3(va), sp, na_w), r3(ga),
                            na_heads, na_dim, na_dim, tq=sp, hps=na_heads)
            ob = _attention(r3(q_cat), _seg3(r3(k_cat), sp, mla_heads * 2 * LANES)
                            + _seg3(r3(v_mla), sp, mla_w),
                            r3(gb), mla_heads, 2 * LANES, mla_v, tq=sp, hps=mla_heads)
            xp = _merge([oa.reshape(tp, -1), ob.reshape(tp, -1)], w_out_even_b, i, xp, ada_l, row_p, g_post[l])

            qa, ka, va, ga, cq, ckvkpe, gb = _proj(
                xs, ss, g_pre[l], ada_l, row_s, weights, [BF16, BF16, BF16, BF16, F32, F32, BF16], steps=4,
                out_scales=even_scales)
            q_cat = _mla_q(cq, ss, mla_g_q[i], w_uq_cat, mla_scale * LOG2E, rope_tables=rope_mla,
                           rope_half=half_mla)
            k_cat, v_mla = _mla_kv(
                ckvkpe, pl.BlockSpec((tm, kv_lora), lambda i_: (i_, 0)),
                ckvkpe, pl.BlockSpec((tm, LANES), lambda i_: (i_, kv_lora // LANES)),
                ts, ss, w_uk, w_uv, g_kv=mla_g_kv[i], rope_tables=rope_mla, rope_half=half_mla)
            tiles_past = past // tm
            kc_cat, vc_mla = _mla_kv(
                cache_mla_ckv, pl.BlockSpec((None, None, tm, kv_lora),
                                            lambda i_: (i_ // tiles_past, i, i_ % tiles_past, 0)),
                cache_kpe_pad, pl.BlockSpec((None, None, tm, LANES),
                                            lambda i_: (i_ // tiles_past, i, i_ % tiles_past, 0)),
                bs * past, past, w_uk, w_uv)
            r3 = lambda a: a.reshape(bs, ss, -1)
            oa = _na_latent(r3(qa), r3(ka), r3(va), ck_na, cv_na, i, bias_rows, r3(ga), na_heads, na_dim,
                            starts, variants)
            c3 = lambda a: a.reshape(bs, past, -1)
            ob = _attention_latent(
                r3(q_cat),
                c3(kc_cat), pl.BlockSpec((None, past, 2 * LANES), lambda bi, h: (bi, 0, h)),
                c3(vc_mla), pl.BlockSpec((None, past, mla_v), lambda bi, h: (bi, 0, h)),
                r3(k_cat), r3(v_mla), r3(gb), mla_heads, 2 * LANES, mla_v)
            xs = _merge([oa.reshape(ts, -1), ob.reshape(ts, -1)], w_out_even_b, i, xs, ada_l, row_s, g_post[l])
        else:
            lam_init = 0.8 - 0.6 * math.exp(-0.3 * l)
            weights = [(w_odd, i, k * diff_w, diff_w) for k in range(4)]
            dh = 2 * diff_d

            q, new_dk, new_dv, g = _proj(
                xp, sp, g_pre[l], ada_l, row_p, weights, [BF16, F32, F32, BF16], steps=4,
                out_scales=odd_scales, cache_slots={1: (i, n_odd, new_dk, 0), 2: (i, n_odd, new_dv, 0)})
            r3 = lambda a: a.reshape(bp, sp, -1)
            o = _attention(r3(q), _seg4(new_dk, i, sp, diff_w) + _seg4(new_dv, i, sp, diff_w), r3(g),
                           diff_heads, dh, dh, tq=sp, hps=diff_heads, diff=True,
                           lam_params=diff_lambda[i], g_sub=diff_g[i], lam_init=lam_init)
            xp = _merge([o.reshape(tp, -1)], w_out_odd_b, i, xp, ada_l, row_p, g_post[l])

            q, k, v, g = _proj(xs, ss, g_pre[l], ada_l, row_s, weights, [BF16, BF16, BF16, BF16], steps=4,
                               rope_flags=(True, True, False, False), rope_tables=rope_diff,
                               rope_half=half_diff, out_scales=odd_scales)
            r3 = lambda a: a.reshape(bs, ss, -1)
            ctx_spec = pl.BlockSpec((None, None, past, dh), lambda bi, h: (bi, i, 0, h))
            o = _attention_latent(
                r3(q), ck_diff, ctx_spec, cv_diff, ctx_spec, r3(k), r3(v), r3(g), diff_heads, dh, dh,
                diff=True, lam_params=diff_lambda[i], g_sub=diff_g[i], lam_init=lam_init)
            xs = _merge([o.reshape(ts, -1)], w_out_odd_b, i, xs, ada_l, row_s, g_post[l])

    return (xp.reshape(bp, sp, d), xs.reshape(bs, ss, d),
            new_na_k.reshape(bp, n_even, sp, na_heads, na_dim),
            new_na_v.reshape(bp, n_even, sp, na_heads, na_dim),
            new_ckv, new_kpe,
            new_dk.reshape(bp, n_odd, sp, diff_heads, 2 * diff_d),
            new_dv.reshape(bp, n_odd, sp, diff_heads, 2 * diff_d))
```

```python
import functools
import math

import numpy as np
import jax
import jax.numpy as jnp
from jax import lax
from jax.experimental import pallas as pl
from jax.experimental.pallas import tpu as pltpu

GRID_W = 64
WIN_R = 8
WIN_C = 16
EPS = 1e-6
ROPE_BASE = 10000.0
LOG2E = 1.4426950408889634
NEG_BIG = -1e30

LANES = 128
VMEM_LIMIT = 56 * 1024 * 1024

F32 = jnp.float32
BF16 = jnp.bfloat16


def _params(sem, vmem=VMEM_LIMIT):
    return pltpu.CompilerParams(dimension_semantics=sem, vmem_limit_bytes=vmem)


def _rms(x, g):
    return x * lax.rsqrt(jnp.mean(x * x, axis=-1, keepdims=True) + EPS) * g


def _silu(x):
    return x * (1.0 / (1.0 + jnp.exp(-x)))


def _rope(x, cos, sin_lo, sin_hi, half):
    return (x * cos + pltpu.roll(x, LANES - half, 1) * sin_lo
            + pltpu.roll(x, half, 1) * sin_hi)


def _slot_spec(rows, seq, width, slot, col_of):
    return pl.BlockSpec((rows, None, seq, width),
                        lambda *g: (g[0], slot, 0, col_of(*g)))


def _ada_kernel(cond_ref, w_ref, b_ref, o_ref):
    a = _silu(cond_ref[...]).astype(BF16)
    o_ref[...] = jnp.dot(a, w_ref[...].astype(BF16), preferred_element_type=F32) + b_ref[...]


def _ada(cond, w_ada, b_ada):
    depth, d, n = w_ada.shape
    rows = cond.shape[0]
    tn = 1024
    return pl.pallas_call(
        _ada_kernel,
        out_shape=jax.ShapeDtypeStruct((depth, rows, n), F32),
        grid=(depth, n // tn),
        in_specs=[
            pl.BlockSpec((rows, d), lambda l, j: (0, 0)),
            pl.BlockSpec((None, d, tn), lambda l, j: (l, 0, j)),
            pl.BlockSpec((None, 1, tn), lambda l, j: (l, 0, j)),
        ],
        out_specs=pl.BlockSpec((None, rows, tn), lambda l, j: (l, 0, j)),
        compiler_params=_params(("arbitrary", "arbitrary")),
        name="ada",
    )(cond, w_ada, b_ada.reshape(depth, 1, n))


def _proj_kernel(*refs, n_groups, rope_flags, rope_half, out_scales, n_alias, cache_heads, steps):
    refs = refs[n_alias:]
    x0_ref, xn_ref, g_ref, sh0_ref, sc0_ref, shn_ref, scn_ref = refs[:7]
    pos = 7
    if any(rope_flags):
        cos_ref, slo_ref, shi_ref = refs[pos:pos + 3]
        pos += 3
    w_refs = refs[pos:pos + n_groups]
    o_refs = refs[pos + n_groups:pos + 2 * n_groups]
    pos += 2 * n_groups
    c_refs = dict(zip(sorted(cache_heads), refs[pos:pos + len(cache_heads)]))
    h_ref = refs[pos + len(cache_heads)]
    i, j = pl.program_id(0), pl.program_id(1)
    slot = i % 2

    def modulated(x, sh_ref_, sc_ref_):
        return (_rms(x, g_ref[...]) * (1.0 + sc_ref_[...]) + sh_ref_[...]).astype(BF16)

    @pl.when((i == 0) & (j == 0))
    def _():
        h_ref[0] = modulated(x0_ref[...], sh0_ref, sc0_ref)

    h = h_ref[slot]
    for gi in range(n_groups):
        acc = jnp.dot(h, w_refs[gi][...], preferred_element_type=F32)
        if rope_flags[gi]:
            cos, slo, shi = cos_ref[...], slo_ref[...], shi_ref[...]
            parts = [_rope(acc[:, s:s + LANES], cos, slo, shi, rope_half)
                     for s in range(0, acc.shape[1], LANES)]
            acc = jnp.concatenate(parts, axis=1)
        if out_scales[gi] != 1.0:
            acc = acc * out_scales[gi]
        o_refs[gi][...] = acc.astype(o_refs[gi].dtype).reshape(o_refs[gi].shape)
        if gi in c_refs:
            c_ref, heads = c_refs[gi], cache_heads[gi]
            seq = c_ref.shape[1] // heads
            hw = c_ref.shape[2]
            for hl in range(acc.shape[1] // hw):
                hd = j * (acc.shape[1] // hw) + hl
                for bi in range(c_ref.shape[0]):
                    c_ref[bi, pl.ds(hd, seq, stride=heads), :] = acc[bi * seq:(bi + 1) * seq,
                                                                       hl * hw:(hl + 1) * hw]

    rows = h_ref.shape[1] // steps
    r0 = pl.multiple_of(j * rows, rows)
    h_ref[1 - slot, pl.ds(r0, rows), :] = modulated(xn_ref[pl.ds(r0, rows), :], shn_ref, scn_ref)


def _proj(x, seq, g_pre, ada_l, mod_row, weights, out_dtypes, steps, rope_flags=None,
          rope_tables=None, rope_half=0, out_scales=None, cache_slots=None, tm=512):
    t, d = x.shape
    n_groups = len(weights)
    n_tiles = t // tm
    rope_flags = tuple(rope_flags) if rope_flags is not None else (False,) * n_groups
    out_scales = tuple(out_scales) if out_scales is not None else (1.0,) * n_groups
    cache_slots = cache_slots or {}
    tiles_per_seq = max(seq // tm, 1)
    nxt = lambda i: jnp.minimum(i + 1, n_tiles - 1)
    aliased = [(gi, cs[2]) for gi, cs in sorted(cache_slots.items()) if cs[2] is not None]
    in_specs = [pl.BlockSpec(memory_space=pl.ANY) for _ in aliased]
    in_specs += [
        pl.BlockSpec((tm, d), lambda i, j: (0, 0)),
        pl.BlockSpec((tm, d), lambda i, j: (nxt(i), 0)),
        pl.BlockSpec((1, d), lambda i, j: (0, 0)),
        pl.BlockSpec((None, 1, d), lambda i, j: (mod_row(0), 0, 0)),
        pl.BlockSpec((None, 1, d), lambda i, j: (mod_row(0), 0, 1)),
        pl.BlockSpec((None, 1, d), lambda i, j: (mod_row(nxt(i)), 0, 0)),
        pl.BlockSpec((None, 1, d), lambda i, j: (mod_row(nxt(i)), 0, 1)),
    ]
    args = [prev for _, prev in aliased] + [x, x, g_pre.reshape(1, d), ada_l, ada_l, ada_l, ada_l]
    if any(rope_flags):
        for tab in rope_tables:
            in_specs.append(pl.BlockSpec((tm, LANES), lambda i, j: (i % tiles_per_seq, 0)))
            args.append(tab)
    out_specs, out_shapes = [], []
    for gi, ((w, layer, col0, n), dt) in enumerate(zip(weights, out_dtypes)):
        tn = n // steps
        in_specs.append(pl.BlockSpec(
            (None, d, tn), functools.partial(lambda i, j, ly, b0: (ly, 0, b0 + j), ly=layer, b0=col0 // tn)))
        args.append(w)
        if gi in cache_slots and not cache_slots[gi][3]:
            slot, n_slots = cache_slots[gi][:2]
            out_specs.append(_slot_spec(tm // seq, seq, tn, slot, lambda i, j: j))
            out_shapes.append(jax.ShapeDtypeStruct((t // seq, n_slots, seq, n), dt))
        else:
            out_specs.append(pl.BlockSpec((tm, tn), lambda i, j: (i, j)))
            out_shapes.append(jax.ShapeDtypeStruct((t, n), dt))
    merged = [gi for gi in sorted(cache_slots) if cache_slots[gi][3]]
    for gi in merged:
        slot, n_slots, _, heads = cache_slots[gi]
        n = weights[gi][3]
        out_specs.append(_slot_spec(tm // seq, seq * heads, n // heads, slot, lambda i, j: 0))
        out_shapes.append(jax.ShapeDtypeStruct((t // seq, n_slots, seq * heads, n // heads), F32))
    cache_out = lambda gi: n_groups + merged.index(gi) if gi in merged else gi
    return pl.pallas_call(
        functools.partial(_proj_kernel, n_groups=n_groups, rope_flags=rope_flags, rope_half=rope_half,
                          out_scales=out_scales, n_alias=len(aliased),
                          cache_heads={gi: cache_slots[gi][3] for gi in merged}, steps=steps),
        out_shape=out_shapes,
        grid=(n_tiles, steps),
        in_specs=in_specs,
        out_specs=out_specs,
        input_output_aliases={k: cache_out(gi) for k, (gi, _) in enumerate(aliased)},
        scratch_shapes=[pltpu.VMEM((2, tm, d), BF16)],
        compiler_params=_params(("arbitrary", "arbitrary")),
        name="proj",
    )(*args)


def _mla_q_kernel(*refs, rope, rope_half, q_scale):
    if rope:
        cq_ref, g_ref, w_ref, cos_ref, slo_ref, shi_ref, o_ref = refs
    else:
        cq_ref, g_ref, w_ref, o_ref = refs
    n = _rms(cq_ref[...], g_ref[...]).astype(BF16)
    q = jnp.dot(n, w_ref[...], preferred_element_type=F32) * q_scale
    if rope:
        cos, slo, shi = cos_ref[...], slo_ref[...], shi_ref[...]
        parts = []
        for s in range(0, q.shape[1], 2 * LANES):
            parts.append(q[:, s:s + LANES])
            parts.append(_rope(q[:, s + LANES:s + 2 * LANES], cos, slo, shi, rope_half))
        q = jnp.concatenate(parts, axis=1)
    o_ref[...] = q.astype(o_ref.dtype)


def _mla_q(cq, seq, g_q, w_uq_cat, q_scale, rope_tables=None, rope_half=0, tm=512):
    t, r = cq.shape
    n = w_uq_cat.shape[1]
    rope = rope_tables is not None
    tiles_per_seq = max(seq // tm, 1)
    in_specs = [
        pl.BlockSpec((tm, r), lambda i: (i, 0)),
        pl.BlockSpec((1, r), lambda i: (0, 0)),
        pl.BlockSpec((r, n), lambda i: (0, 0)),
    ]
    args = [cq, g_q.reshape(1, r), w_uq_cat]
    if rope:
        for tab in rope_tables:
            in_specs.append(pl.BlockSpec((tm, LANES), lambda i: (i % tiles_per_seq, 0)))
            args.append(tab)
    return pl.pallas_call(
        functools.partial(_mla_q_kernel, rope=rope, rope_half=rope_half, q_scale=q_scale),
        out_shape=jax.ShapeDtypeStruct((t, n), BF16),
        grid=(t // tm,),
        in_specs=in_specs,
        out_specs=pl.BlockSpec((tm, n), lambda i: (i, 0)),
        compiler_params=_params(("arbitrary",)),
        name="mla_q",
    )(*args)


def _mla_kv_kernel(*refs, normalise, rope, rope_half, emit_cache, n_alias, n_heads, nope, kpe_width):
    refs = list(refs[n_alias:])
    ckv_ref, kpe_ref = refs[:2]
    pos = 2
    if normalise:
        g_ref = refs[pos]
        pos += 1
    wk_ref, wv_ref = refs[pos:pos + 2]
    pos += 2
    if rope:
        cos_ref, slo_ref, shi_ref = refs[pos:pos + 3]
        pos += 3
    kcat_ref, v_ref = refs[pos:pos + 2]
    pos += 2
    ckv = ckv_ref[...]
    if normalise:
        ckv = _rms(ckv, g_ref[...])
    kpe = kpe_ref[...]
    kpe = jnp.where(lax.broadcasted_iota(jnp.int32, kpe.shape, 1) < kpe_width, kpe, 0.0)
    if emit_cache:
        ckv_out_ref, kpe_out_ref = refs[pos:pos + 2]
        ckv_out_ref[...] = ckv.reshape(ckv_out_ref.shape)
        kpe_out_ref[...] = kpe[:, :kpe_out_ref.shape[-1]].reshape(kpe_out_ref.shape)
    if rope:
        kpe = _rope(kpe, cos_ref[...], slo_ref[...], shi_ref[...], rope_half)
    cb = ckv.astype(BF16)
    kn = jnp.dot(cb, wk_ref[...], preferred_element_type=F32).astype(BF16)
    v_ref[...] = jnp.dot(cb, wv_ref[...], preferred_element_type=F32).astype(BF16)
    kpe_b = kpe.astype(BF16)
    for h in range(n_heads):
        kcat_ref[:, h * 2 * nope:h * 2 * nope + nope] = kn[:, h * nope:(h + 1) * nope]
        kcat_ref[:, h * 2 * nope + nope:(h + 1) * 2 * nope] = kpe_b


def _mla_kv(ckv_arr, ckv_spec, kpe_arr, kpe_spec, t, seq, w_uk, w_uv, g_kv=None, rope_tables=None,
            rope_half=0, cache=None, kpe_width=64, tm=512):
    lora, nk = w_uk.shape
    nv = w_uv.shape[1]
    nope = LANES
    n_heads = nk // nope
    normalise = g_kv is not None
    rope = rope_tables is not None
    tiles_per_seq = max(seq // tm, 1)
    aliased = [] if cache is None or cache[2] is None else [cache[2], cache[3]]
    in_specs = [pl.BlockSpec(memory_space=pl.ANY) for _ in aliased] + [ckv_spec, kpe_spec]
    args = aliased + [ckv_arr, kpe_arr]
    if normalise:
        in_specs.append(pl.BlockSpec((1, lora), lambda i: (0, 0)))
        args.append(g_kv.reshape(1, lora))
    in_specs += [pl.BlockSpec((lora, nk), lambda i: (0, 0)), pl.BlockSpec((lora, nv), lambda i: (0, 0))]
    args += [w_uk, w_uv]
    if rope:
        for tab in rope_tables:
            in_specs.append(pl.BlockSpec((tm, LANES), lambda i: (i % tiles_per_seq, 0)))
            args.append(tab)
    out_shapes = [jax.ShapeDtypeStruct((t, 2 * nk), BF16), jax.ShapeDtypeStruct((t, nv), BF16)]
    out_specs = [pl.BlockSpec((tm, 2 * nk), lambda i: (i, 0)), pl.BlockSpec((tm, nv), lambda i: (i, 0))]
    if cache is not None:
        slot, n_slots = cache[:2]
        out_shapes += [jax.ShapeDtypeStruct((t // seq, n_slots, seq, lora), F32),
                       jax.ShapeDtypeStruct((t // seq, n_slots, seq, kpe_width), F32)]
        out_specs += [_slot_spec(tm // seq, seq, lora, slot, lambda i: 0),
                      _slot_spec(tm // seq, seq, kpe_width, slot, lambda i: 0)]
    return pl.pallas_call(
        functools.partial(_mla_kv_kernel, normalise=normalise, rope=rope, rope_half=rope_half,
                          emit_cache=cache is not None, n_alias=len(aliased), n_heads=n_heads, nope=nope,
                          kpe_width=kpe_width),
        out_shape=out_shapes,
        grid=(t // tm,),
        in_specs=in_specs,
        out_specs=out_specs,
        input_output_aliases={k: 2 + k for k in range(len(aliased))},
        compiler_params=_params(("arbitrary",)),
        name="mla_kv",
    )(*args)


def _softmax_parts(s_list):
    m = functools.reduce(jnp.maximum, [jnp.max(s, axis=-1, keepdims=True) for s in s_list])
    p_list = [jnp.exp2(s - m) for s in s_list]
    l = functools.reduce(lambda a, b: a + b, [jnp.sum(p, axis=-1, keepdims=True) for p in p_list])
    return p_list, l


def _qk(q, k):
    return lax.dot_general(q, k, (((1,), (1,)), ((), ())), preferred_element_type=F32)


def _diff_lambda(lam_ref, lam_init):
    lp = lam_ref[...]
    return (jnp.exp(jnp.sum(lp[0:1] * lp[1:2], axis=-1, keepdims=True))
            - jnp.exp(jnp.sum(lp[2:3] * lp[3:4], axis=-1, keepdims=True)) + lam_init)


def _softmax_weights(s_ref, m_ref, lam):
    ps = [jnp.exp2(s_ref[mp] - m_ref[mp]) for mp in range(s_ref.shape[0])]
    ls = [jnp.sum(p, axis=-1, keepdims=True) for p in ps]
    w = ps[0].astype(BF16)
    if len(ps) == 2:
        w = w - (lam * ls[0] / ls[1]).astype(BF16) * ps[1].astype(BF16)
    return w, 1.0 / ls[0]


def _attn_kernel(*refs, diff, lam_init, hps, dq, dv):
    q_ref, k_ref, v_ref, g_ref = refs[:4]
    pos = 4
    if diff:
        lam_ref, gsub_ref = refs[pos:pos + 2]
        pos += 2
        lam = _diff_lambda(lam_ref, lam_init)
    o_ref = refs[pos]
    s_ref, w_ref = refs[pos + 1:pos + 3]
    n_maps = 2 if diff else 1
    d = dq // n_maps
    for hh in range(hps):
        for mp in range(n_maps):
            c0 = hh * dq + mp * d
            s_ref[hh * n_maps + mp] = _qk(q_ref[:, c0:c0 + d].astype(BF16), k_ref[:, c0:c0 + d].astype(BF16))
    inv_l = []
    for hh in range(hps):
        if not diff:
            (p,), l = _softmax_parts([s_ref[hh]])
            w_ref[hh] = p.astype(BF16)
            inv_l.append(1.0 / l)
        else:
            (p1,), l1 = _softmax_parts([s_ref[2 * hh]])
            (p2,), l2 = _softmax_parts([s_ref[2 * hh + 1]])
            w_ref[hh] = (p1 * (1.0 / l1) - p2 * (lam / l2)).astype(BF16)
    for hh in range(hps):
        cols = slice(hh * dv, (hh + 1) * dv)
        o = jnp.dot(w_ref[hh], v_ref[:, cols].astype(BF16), preferred_element_type=F32)
        if diff:
            o = _rms(o, gsub_ref[...]) * (1.0 - lam_init)
        else:
            o = o * inv_l[hh]
        o_ref[:, cols] = (o * _silu(g_ref[:, cols].astype(F32))).astype(o_ref.dtype)


def _attention(q, seg, gate, n_heads, dq, dv, tq, hps=1, diff=False, lam_params=None,
               g_sub=None, lam_init=0.0):
    b, sq, _ = q.shape
    k_arr, k_spec, v_arr, v_spec = seg
    sk = k_spec.block_shape[-2]
    n_maps = 2 if diff else 1
    in_specs = [pl.BlockSpec((None, tq, hps * dq), lambda bi, h, qi: (bi, qi, h)), k_spec, v_spec,
                pl.BlockSpec((None, tq, hps * dv), lambda bi, h, qi: (bi, qi, h))]
    args = [q, k_arr, v_arr, gate]
    if diff:
        in_specs += [pl.BlockSpec(lam_params.shape, lambda bi, h, qi: (0, 0)),
                     pl.BlockSpec((1, dv), lambda bi, h, qi: (0, 0))]
        args += [lam_params, g_sub.reshape(1, dv)]
    return pl.pallas_call(
        functools.partial(_attn_kernel, diff=diff, lam_init=lam_init, hps=hps, dq=dq, dv=dv),
        out_shape=jax.ShapeDtypeStruct((b, sq, n_heads * dv), BF16),
        grid=(b, n_heads // hps, sq // tq),
        in_specs=in_specs,
        out_specs=pl.BlockSpec((None, tq, hps * dv), lambda bi, h, qi: (bi, qi, h)),
        scratch_shapes=[pltpu.VMEM((hps * n_maps, tq, sk), F32), pltpu.VMEM((hps, tq, sk), BF16)],
        compiler_params=_params(("arbitrary", "arbitrary", "arbitrary")),
        name="attn_diff" if diff else "attn",
    )(*args)


def _seg3(arr, sk, width):
    return arr, pl.BlockSpec((None, sk, width), lambda bi, h, qi: (bi, 0, h))


def _seg4(arr, layer, sk, width):
    return arr, pl.BlockSpec((None, None, sk, width), lambda bi, h, qi: (bi, layer, 0, h))


def _attn_latent_kernel(*refs, diff, lam_init, tq, dq, dv):
    q_ref, kc_ref, vc_ref, k_ref, v_ref, g_ref = refs[:6]
    pos = 6
    lam = None
    if diff:
        lam_ref, gsub_ref = refs[pos:pos + 2]
        pos += 2
        lam = _diff_lambda(lam_ref, lam_init)
    o_ref = refs[pos]
    s_bufs = refs[pos + 1:pos + 3]
    m_bufs = refs[pos + 3:pos + 5]
    kcb_ref, vcb_ref = refs[pos + 5:pos + 7]
    n_maps = 2 if diff else 1
    d = dq // n_maps
    past = kc_ref.shape[0]
    n_blocks = q_ref.shape[0] // tq

    kcb_ref[...] = kc_ref[...].astype(BF16)
    vcb_ref[...] = vc_ref[...].astype(BF16)

    def scores(qi, slot):
        q = q_ref[qi * tq:(qi + 1) * tq, :]
        for mp in range(n_maps):
            qm = q[:, mp * d:(mp + 1) * d]
            s_c = _qk(qm, kcb_ref[:, mp * d:(mp + 1) * d])
            s_l = _qk(qm, k_ref[:, mp * d:(mp + 1) * d])
            s_bufs[slot][mp, :, 0:past] = s_c
            s_bufs[slot][mp, :, past:] = s_l
            m_bufs[slot][mp] = jnp.maximum(jnp.max(s_c, axis=-1, keepdims=True),
                                           jnp.max(s_l, axis=-1, keepdims=True))

    def finish(qi, slot):
        rows = slice(qi * tq, (qi + 1) * tq)
        w, inv_l = _softmax_weights(s_bufs[slot], m_bufs[slot], lam)
        o = (jnp.dot(w[:, :past], vcb_ref[...], preferred_element_type=F32)
             + jnp.dot(w[:, past:], v_ref[...], preferred_element_type=F32)) * inv_l
        if diff:
            o = _rms(o, gsub_ref[...]) * (1.0 - lam_init)
        o_ref[rows, :] = (o * _silu(g_ref[rows, :].astype(F32))).astype(o_ref.dtype)

    scores(0, 0)
    for t in range(n_blocks):
        if t + 1 < n_blocks:
            scores(t + 1, (t + 1) % 2)
        finish(t, t % 2)


def _attention_latent(q, kc, kc_spec, vc, vc_spec, k, v, gate, n_heads, dq, dv, tq=256, diff=False,
                      lam_params=None, g_sub=None, lam_init=0.0):
    b, s, _ = q.shape
    past = kc_spec.block_shape[-2]
    n_maps = 2 if diff else 1
    head = lambda w: pl.BlockSpec((None, s, w), lambda bi, h: (bi, 0, h))
    in_specs = [head(dq), kc_spec, vc_spec, head(dq), head(dv), head(dv)]
    args = [q, kc, vc, k, v, gate]
    if diff:
        in_specs += [pl.BlockSpec(lam_params.shape, lambda bi, h: (0, 0)),
                     pl.BlockSpec((1, dv), lambda bi, h: (0, 0))]
        args += [lam_params, g_sub.reshape(1, dv)]
    return pl.pallas_call(
        functools.partial(_attn_latent_kernel, diff=diff, lam_init=lam_init, tq=tq,
                          dq=dq, dv=dv),
        out_shape=jax.ShapeDtypeStruct((b, s, n_heads * dv), BF16),
        grid=(b, n_heads),
        in_specs=in_specs,
        out_specs=head(dv),
        scratch_shapes=[pltpu.VMEM((n_maps, tq, past + s), F32), pltpu.VMEM((n_maps, tq, past + s), F32),
                        pltpu.VMEM((n_maps, tq, 1), F32), pltpu.VMEM((n_maps, tq, 1), F32),
                        pltpu.VMEM((past, dq), BF16), pltpu.VMEM((past, dv), BF16)],
        compiler_params=_params(("arbitrary", "arbitrary")),
        name="attn_latent_diff" if diff else "attn_latent",
    )(*args)


NA_Q_ROWS = 4
NA_K_ROWS = WIN_R + NA_Q_ROWS
NA_K_PAIRS = NA_K_ROWS * GRID_W // LANES
RPB_W = 2 * WIN_C - 1


def _na_blocks(rows):
    assert 2 * GRID_W == LANES and rows >= NA_K_ROWS
    wr = min(WIN_R, rows)
    r = np.arange(rows)
    rs = np.clip(r - wr // 2, 0, rows - wr)
    starts, variants, keys = [], [], {}
    for t in range(rows // NA_Q_ROWS):
        r0 = t * NA_Q_ROWS
        ws = int(np.clip(r0 - wr // 2, 0, rows - NA_K_ROWS))
        sig = (r0 - ws,) + tuple(int(rs[r0 + dq] - ws) for dq in range(NA_Q_ROWS))
        assert all(0 <= f and f + wr <= NA_K_ROWS for f in sig[1:])
        variants.append(keys.setdefault(sig, len(keys)))
        starts.append(ws)
    sigs = sorted(keys, key=keys.get)
    idx = np.zeros((len(sigs), NA_Q_ROWS, NA_K_PAIRS, LANES), np.int32)
    kind = np.ones(idx.shape, np.int32)
    for vi, sig in enumerate(sigs):
        for dq in range(NA_Q_ROWS):
            for j in range(NA_K_ROWS):
                lanes = slice((j % 2) * GRID_W, (j % 2) * GRID_W + RPB_W)
                if sig[1 + dq] <= j < sig[1 + dq] + wr:
                    dr = j - (sig[0] + dq)
                    idx[vi, dq, j // 2, lanes] = (dr + WIN_R - 1) * RPB_W + np.arange(RPB_W)
                    kind[vi, dq, j // 2, lanes] = 0
                else:
                    kind[vi, dq, j // 2, (j % 2) * GRID_W:(j % 2 + 1) * GRID_W] = 2
    return starts, variants, idx, kind


def _na_kernel(q_ref, k_ref, v_ref, kc_ref, vc_ref, rows_ref, g_ref, o_ref, bias_ref, *, starts, variants):
    nq = NA_Q_ROWS * GRID_W
    nk = NA_K_ROWS * GRID_W

    @pl.when(pl.program_id(1) == 0)
    def _():
        ci = lax.broadcasted_iota(jnp.int32, (GRID_W, LANES), 0)
        kc = lax.broadcasted_iota(jnp.int32, (GRID_W, LANES), 1) & (GRID_W - 1)
        cs = jnp.clip(ci - WIN_C // 2, 0, GRID_W - WIN_C)
        col_mask = jnp.where((kc >= cs) & (kc < cs + WIN_C), 0.0, NEG_BIG)
        for vi in range(bias_ref.shape[0]):
            for dq in range(NA_Q_ROWS):
                vecs = rows_ref[vi, dq]
                for jp in range(NA_K_PAIRS):
                    w = jnp.broadcast_to(vecs[jp:jp + 1, :], (GRID_W, LANES))
                    tile = pltpu.roll(w, LANES - (WIN_C - 1), 1, stride=1, stride_axis=0)
                    bias_ref[vi, dq * GRID_W:(dq + 1) * GRID_W, jp * LANES:(jp + 1) * LANES] = (
                        (tile + col_mask) * LOG2E)

    kc = kc_ref[...].astype(BF16)
    vc = vc_ref[...].astype(BF16)

    def body(t, carry):
        start = jnp.int32(starts[0])
        var = jnp.int32(variants[0])
        for ti in range(1, len(starts)):
            start = jnp.where(t == ti, jnp.int32(starts[ti]), start)
            var = jnp.where(t == ti, jnp.int32(variants[ti]), var)
        q0 = pl.multiple_of(t * nq, nq)
        k0 = pl.multiple_of(start * GRID_W, GRID_W)
        q = q_ref[pl.ds(q0, nq), :]
        kw = k_ref[pl.ds(k0, nk), :]
        vw = v_ref[pl.ds(k0, nk), :]
        s_w = _qk(q, kw) + bias_ref[var]
        s_c = _qk(q, kc)
        (p_w, p_c), l = _softmax_parts([s_w, s_c])
        o = (jnp.dot(p_w.astype(BF16), vw, preferred_element_type=F32)
             + jnp.dot(p_c.astype(BF16), vc, preferred_element_type=F32)) / l
        g = g_ref[pl.ds(q0, nq), :].astype(F32)
        o_ref[pl.ds(q0, nq), :] = (o * _silu(g)).astype(o_ref.dtype)
        return carry

    lax.fori_loop(0, len(starts), body, 0, unroll=8)


def _na_latent(q, k, v, cache_k, cache_v, layer, bias_rows, gate, n_heads, d, starts, variants):
    b, s, _ = q.shape
    past = cache_k.shape[2]
    n_var = bias_rows.shape[1]
    head = pl.BlockSpec((None, s, d), lambda h, bi: (bi, 0, h))
    ctx = pl.BlockSpec((None, None, past, d), lambda h, bi: (bi, layer, 0, h))
    return pl.pallas_call(
        functools.partial(_na_kernel, starts=tuple(starts), variants=tuple(variants)),
        out_shape=jax.ShapeDtypeStruct((b, s, n_heads * d), BF16),
        grid=(n_heads, b),
        in_specs=[head, head, head, ctx, ctx,
                  pl.BlockSpec((None,) + bias_rows.shape[1:], lambda h, bi: (h, 0, 0, 0, 0)),
                  head],
        out_specs=head,
        scratch_shapes=[pltpu.VMEM((n_var, NA_Q_ROWS * GRID_W, NA_K_ROWS * GRID_W), F32)],
        compiler_params=_params(("arbitrary", "arbitrary")),
        name="na_latent",
    )(q, k, v, cache_k, cache_v, bias_rows, gate)


def _merge_kernel(*refs, n_in):
    o_refs = refs[:n_in]
    w_ref, x_ref, gate_ref, g_ref, out_ref = refs[n_in:]
    y = None
    row = 0
    for o_ref in o_refs:
        width = o_ref.shape[1]
        part = jnp.dot(o_ref[...], w_ref[row:row + width, :], preferred_element_type=F32)
        y = part if y is None else y + part
        row += width
    out_ref[...] = x_ref[...] + gate_ref[...] * _rms(y, g_ref[...])


def _merge(o_list, w_out, layer, x, ada_l, mod_row, g_post, tm=512):
    t, d = x.shape
    in_specs = [pl.BlockSpec((tm, o.shape[1]), lambda i: (i, 0)) for o in o_list]
    in_specs += [
        pl.BlockSpec((None,) + w_out.shape[1:], lambda i: (layer, 0, 0)),
        pl.BlockSpec((tm, d), lambda i: (i, 0)),
        pl.BlockSpec((None, 1, d), lambda i: (mod_row(i), 0, 2)),
        pl.BlockSpec((1, d), lambda i: (0, 0)),
    ]
    return pl.pallas_call(
        functools.partial(_merge_kernel, n_in=len(o_list)),
        out_shape=jax.ShapeDtypeStruct((t, d), F32),
        grid=(t // tm,),
        in_specs=in_specs,
        out_specs=pl.BlockSpec((tm, d), lambda i: (i, 0)),
        compiler_params=_params(("arbitrary",)),
        name="merge",
    )(*o_list, w_out, x, ada_l, g_post.reshape(1, d))


def _rope_tables(s, n):
    a = n // 2
    half = a // 2
    t = jnp.arange(s)
    inv = ROPE_BASE ** (-jnp.arange(half, dtype=F32) / half)
    ang_r = (t // GRID_W).astype(F32)[:, None] * inv[None, :]
    ang_c = (t % GRID_W).astype(F32)[:, None] * inv[None, :]
    cos = jnp.concatenate([jnp.cos(ang_r)] * 2 + [jnp.cos(ang_c)] * 2, axis=-1)
    sin = jnp.concatenate([jnp.sin(ang_r)] * 2 + [jnp.sin(ang_c)] * 2, axis=-1)
    low = (np.arange(n) % a) < half
    sin_lo = jnp.where(low[None, :], -sin, 0.0)
    sin_hi = jnp.where(low[None, :], 0.0, sin)
    pad = LANES - n
    if pad:
        cos = jnp.pad(cos, ((0, 0), (0, pad)), constant_values=1.0)
        sin_lo = jnp.pad(sin_lo, ((0, 0), (0, pad)))
        sin_hi = jnp.pad(sin_hi, ((0, 0), (0, pad)))
    return (cos, sin_lo, sin_hi), half


def kernel(x_prompt, x_sample, cache_na_k, cache_na_v, cache_mla_ckv, cache_mla_kpe, cache_diff_k, cache_diff_v, c, c_ctx, w_ada, b_ada, g_pre, g_post, w_in_even, w_out_even, na_rpb, mla_g_q, mla_w_uq, mla_g_kv, mla_w_ukv, w_in_odd, w_out_odd, diff_lambda, diff_g):
    bp, sp, d = x_prompt.shape
    bs, ss, _ = x_sample.shape
    depth = w_ada.shape[0]
    n_even, n_odd = w_in_even.shape[0], w_in_odd.shape[0]
    past = cache_na_k.shape[2]
    na_heads, na_dim = cache_na_k.shape[3:]
    na_w = na_heads * na_dim
    kv_lora = cache_mla_ckv.shape[3]
    rope_w = cache_mla_kpe.shape[3]
    q_lora = mla_g_q.shape[1]
    diff_heads = cache_diff_k.shape[3]
    diff_w = diff_heads * cache_diff_k.shape[4]
    diff_d = diff_w // (2 * diff_heads)
    mla_w = w_out_even.shape[1] - na_w
    mla_v = mla_nope = LANES
    mla_heads = mla_w // mla_v
    na_scale = na_dim ** -0.5
    mla_scale = (mla_nope + rope_w) ** -0.5
    diff_scale = diff_d ** -0.5
    tp, ts = bp * sp, bs * ss
    tm = 512
    even_scales = (na_scale * LOG2E, 1.0, 1.0, 1.0, 1.0, 1.0, 1.0)
    odd_scales = (diff_scale * LOG2E, 1.0, 1.0, 1.0)

    ctx_row = bs
    cond = jnp.concatenate([c, c_ctx[None, :], jnp.zeros((16 - bs - 1, d), F32)], axis=0)
    ada = _ada(cond, w_ada, b_ada)
    row_p = lambda i: ctx_row
    row_s = lambda i: (i * tm) // ss

    rope_diff, half_diff = _rope_tables(ss, diff_d)
    rope_mla, half_mla = _rope_tables(ss, rope_w)

    starts, variants, rows_idx, rows_kind = _na_blocks(ss // GRID_W)

    xp = x_prompt.reshape(tp, d)
    xs = x_sample.reshape(ts, d)
    ck_na = cache_na_k.reshape(bs, -1, past, na_w)
    cv_na = cache_na_v.reshape(bs, -1, past, na_w)
    ck_diff = cache_diff_k.reshape(bs, -1, past, diff_w)
    cv_diff = cache_diff_v.reshape(bs, -1, past, diff_w)
    cache_kpe_pad = jnp.pad(cache_mla_kpe, ((0, 0), (0, 0), (0, 0), (0, LANES - rope_w)))

    kv_start = 4 * na_w + q_lora
    kv_end = kv_start + kv_lora + rope_w
    w_even = w_in_even.astype(BF16)
    w_gb = w_in_even[:, :, kv_end:].astype(BF16)
    w_odd = w_in_odd.astype(BF16)
    w_out_even_b = w_out_even.astype(BF16)
    w_out_odd_b = w_out_odd.astype(BF16)

    new_na_k = new_na_v = new_ckv = new_kpe = new_dk = new_dv = None
    for l in range(depth):
        i = l // 2
        ada_l = ada[l].reshape(16, 1, 3 * d)
        if l % 2 == 0:
            weights = [(w_even, i, k * na_w, na_w) for k in range(4)]
            weights += [(w_even, i, 4 * na_w, q_lora), (w_even, i, kv_start, 512), (w_gb, i, 0, mla_w)]
            w_uq = mla_w_uq[i].reshape(q_lora, mla_heads, mla_nope + rope_w)
            w_uq_cat = jnp.pad(w_uq, ((0, 0), (0, 0), (0, 2 * LANES - mla_nope - rope_w)))
            w_uq_cat = w_uq_cat.reshape(q_lora, mla_heads * 2 * LANES).astype(BF16)
            w_ukv = mla_w_ukv[i].reshape(kv_lora, mla_heads, mla_nope + mla_v)
            w_uk = w_ukv[:, :, :mla_nope].reshape(kv_lora, mla_heads * mla_nope).astype(BF16)
            w_uv = w_ukv[:, :, mla_nope:].reshape(kv_lora, mla_heads * mla_v).astype(BF16)
            rpb_flat = na_rpb[i].reshape(na_heads, -1)
            bias_rows = jnp.where(rows_kind[None] == 0, rpb_flat[:, rows_idx],
                                  jnp.where(rows_kind[None] == 1, 0.0, NEG_BIG))

            qa, ka, va, ga, cq, ckvkpe, gb, new_na_k, new_na_v = _proj(
                xp, sp, g_pre[l], ada_l, row_p, weights, [BF16, BF16, BF16, BF16, F32, F32, BF16], steps=4,
                out_scales=even_scales, cache_slots={1: (i, n_even, new_na_k, na_heads),
                                                     2: (i, n_even, new_na_v, na_heads)})
            q_cat = _mla_q(cq, sp, mla_g_q[i], w_uq_cat, mla_scale * LOG2E)
            k_cat, v_mla, new_ckv, new_kpe = _mla_kv(
                ckvkpe, pl.BlockSpec((tm, kv_lora), lambda i_: (i_, 0)),
                ckvkpe, pl.BlockSpec((tm, LANES), lambda i_: (i_, kv_lora // LANES)),
                tp, sp, w_uk, w_uv, g_kv=mla_g_kv[i], cache=(i, n_even, new_ckv, new_kpe), kpe_width=rope_w)
            r3 = lambda a: a.reshape(bp, sp, -1)
            oa = _attention(r3(qa), _seg3(r3(ka), sp, na_w) + _seg3(r3(va), sp, na_w), r3(ga),
                            na_heads, na_dim, na_dim, tq=sp, hps=na_heads)
            ob = _attention(r3(q_cat), _seg3(r3(k_cat), sp, mla_heads * 2 * LANES)
                            + _seg3(r3(v_mla), sp, mla_w),
                            r3(gb), mla_heads, 2 * LANES, mla_v, tq=sp, hps=mla_heads)
            xp = _merge([oa.reshape(tp, -1), ob.reshape(tp, -1)], w_out_even_b, i, xp, ada_l, row_p, g_post[l])

            qa, ka, va, ga, cq, ckvkpe, gb = _proj(
                xs, ss, g_pre[l], ada_l, row_s, weights, [BF16, BF16, BF16, BF16, F32, F32, BF16], steps=4,
                out_scales=even_scales)
            q_cat = _mla_q(cq, ss, mla_g_q[i], w_uq_cat, mla_scale * LOG2E, rope_tables=rope_mla,
                           rope_half=half_mla)
            k_cat, v_mla = _mla_kv(
                ckvkpe, pl.BlockSpec((tm, kv_lora), lambda i_: (i_, 0)),
                ckvkpe, pl.BlockSpec((tm, LANES), lambda i_: (i_, kv_lora // LANES)),
                ts, ss, w_uk, w_uv, g_kv=mla_g_kv[i], rope_tables=rope_mla, rope_half=half_mla,
                kpe_width=rope_w)
            tiles_past = past // tm
            kc_cat, vc_mla = _mla_kv(
                cache_mla_ckv, pl.BlockSpec((None, None, tm, kv_lora),
                                            lambda i_: (i_ // tiles_past, i, i_ % tiles_past, 0)),
                cache_kpe_pad, pl.BlockSpec((None, None, tm, LANES),
                                            lambda i_: (i_ // tiles_past, i, i_ % tiles_past, 0)),
                bs * past, past, w_uk, w_uv, kpe_width=rope_w)
            r3 = lambda a: a.reshape(bs, ss, -1)
            oa = _na_latent(r3(qa), r3(ka), r3(va), ck_na, cv_na, i, bias_rows, r3(ga), na_heads, na_dim,
                            starts, variants)
            c3 = lambda a: a.reshape(bs, past, -1)
            ob = _attention_latent(
                r3(q_cat),
                c3(kc_cat), pl.BlockSpec((None, past, 2 * LANES), lambda bi, h: (bi, 0, h)),
                c3(vc_mla), pl.BlockSpec((None, past, mla_v), lambda bi, h: (bi, 0, h)),
                r3(k_cat), r3(v_mla), r3(gb), mla_heads, 2 * LANES, mla_v)
            xs = _merge([oa.reshape(ts, -1), ob.reshape(ts, -1)], w_out_even_b, i, xs, ada_l, row_s, g_post[l])
        else:
            lam_init = 0.8 - 0.6 * math.exp(-0.3 * l)
            weights = [(w_odd, i, k * diff_w, diff_w) for k in range(4)]
            dh = 2 * diff_d

            q, new_dk, new_dv, g = _proj(
                xp, sp, g_pre[l], ada_l, row_p, weights, [BF16, F32, F32, BF16], steps=4,
                out_scales=odd_scales, cache_slots={1: (i, n_odd, new_dk, 0), 2: (i, n_odd, new_dv, 0)})
            r3 = lambda a: a.reshape(bp, sp, -1)
            o = _attention(r3(q), _seg4(new_dk, i, sp, diff_w) + _seg4(new_dv, i, sp, diff_w), r3(g),
                           diff_heads, dh, dh, tq=sp, hps=diff_heads, diff=True,
                           lam_params=diff_lambda[i], g_sub=diff_g[i], lam_init=lam_init)
            xp = _merge([o.reshape(tp, -1)], w_out_odd_b, i, xp, ada_l, row_p, g_post[l])

            q, k, v, g = _proj(xs, ss, g_pre[l], ada_l, row_s, weights, [BF16, BF16, BF16, BF16], steps=4,
                               rope_flags=(True, True, False, False), rope_tables=rope_diff,
                               rope_half=half_diff, out_scales=odd_scales)
            r3 = lambda a: a.reshape(bs, ss, -1)
            ctx_spec = pl.BlockSpec((None, None, past, dh), lambda bi, h: (bi, i, 0, h))
            o = _attention_latent(
                r3(q), ck_diff, ctx_spec, cv_diff, ctx_spec, r3(k), r3(v), r3(g), diff_heads, dh, dh,
                diff=True, lam_params=diff_lambda[i], g_sub=diff_g[i], lam_init=lam_init)
            xs = _merge([o.reshape(ts, -1)], w_out_odd_b, i, xs, ada_l, row_s, g_post[l])

    return (xp.reshape(bp, sp, d), xs.reshape(bs, ss, d),
            new_na_k.reshape(bp, n_even, sp, na_heads, na_dim),
            new_na_v.reshape(bp, n_even, sp, na_heads, na_dim),
            new_ckv, new_kpe,
            new_dk.reshape(bp, n_odd, sp, diff_heads, 2 * diff_d),
            new_dv.reshape(bp, n_odd, sp, diff_heads, 2 * diff_d))
```

```python
import functools
import math

import numpy as np
import jax
import jax.numpy as jnp
from jax import lax
from jax.experimental import pallas as pl
from jax.experimental.pallas import tpu as pltpu

GRID_W = 64
WIN_R = 8
WIN_C = 16
EPS = 1e-6
ROPE_BASE = 10000.0
LOG2E = 1.4426950408889634
NEG_BIG = -1e30

LANES = 128
VMEM_LIMIT = 56 * 1024 * 1024

F32 = jnp.float32
BF16 = jnp.bfloat16


def _params(sem, vmem=VMEM_LIMIT):
    return pltpu.CompilerParams(dimension_semantics=sem, vmem_limit_bytes=vmem)


def _rms(x, g):
    return x * lax.rsqrt(jnp.mean(x * x, axis=-1, keepdims=True) + EPS) * g


def _silu(x):
    return x * (1.0 / (1.0 + jnp.exp(-x)))


def _rope(x, cos, sin_lo, sin_hi, half):
    return (x * cos + pltpu.roll(x, LANES - half, 1) * sin_lo
            + pltpu.roll(x, half, 1) * sin_hi)


def _slot_spec(rows, seq, width, slot, col_of):
    return pl.BlockSpec((rows, None, seq, width),
                        lambda *g: (g[0], slot, 0, col_of(*g)))


def _ada_kernel(cond_ref, w_ref, b_ref, o_ref):
    a = _silu(cond_ref[...]).astype(BF16)
    o_ref[...] = jnp.dot(a, w_ref[...].astype(BF16), preferred_element_type=F32) + b_ref[...]


def _ada(cond, w_ada, b_ada):
    depth, d, n = w_ada.shape
    rows = cond.shape[0]
    tn = 1024
    return pl.pallas_call(
        _ada_kernel,
        out_shape=jax.ShapeDtypeStruct((depth, rows, n), F32),
        grid=(depth, n // tn),
        in_specs=[
            pl.BlockSpec((rows, d), lambda l, j: (0, 0)),
            pl.BlockSpec((None, d, tn), lambda l, j: (l, 0, j)),
            pl.BlockSpec((None, 1, tn), lambda l, j: (l, 0, j)),
        ],
        out_specs=pl.BlockSpec((None, rows, tn), lambda l, j: (l, 0, j)),
        compiler_params=_params(("arbitrary", "arbitrary")),
        name="ada",
    )(cond, w_ada, b_ada.reshape(depth, 1, n))


def _proj_kernel(*refs, n_groups, rope_flags, rope_half, out_scales, n_alias, cache_heads, steps):
    refs = refs[n_alias:]
    x0_ref, xn_ref, g_ref, sh0_ref, sc0_ref, shn_ref, scn_ref = refs[:7]
    pos = 7
    if any(rope_flags):
        cos_ref, slo_ref, shi_ref = refs[pos:pos + 3]
        pos += 3
    w_refs = refs[pos:pos + n_groups]
    o_refs = refs[pos + n_groups:pos + 2 * n_groups]
    pos += 2 * n_groups
    c_refs = dict(zip(sorted(cache_heads), refs[pos:pos + len(cache_heads)]))
    h_ref = refs[pos + len(cache_heads)]
    i, j = pl.program_id(0), pl.program_id(1)
    slot = i % 2

    def modulated(x, sh_ref_, sc_ref_):
        return (_rms(x, g_ref[...]) * (1.0 + sc_ref_[...]) + sh_ref_[...]).astype(BF16)

    @pl.when((i == 0) & (j == 0))
    def _():
        h_ref[0] = modulated(x0_ref[...], sh0_ref, sc0_ref)

    h = h_ref[slot]
    for gi in range(n_groups):
        acc = jnp.dot(h, w_refs[gi][...], preferred_element_type=F32)
        if rope_flags[gi]:
            cos, slo, shi = cos_ref[...], slo_ref[...], shi_ref[...]
            parts = [_rope(acc[:, s:s + LANES], cos, slo, shi, rope_half)
                     for s in range(0, acc.shape[1], LANES)]
            acc = jnp.concatenate(parts, axis=1)
        if out_scales[gi] != 1.0:
            acc = acc * out_scales[gi]
        o_refs[gi][...] = acc.astype(o_refs[gi].dtype).reshape(o_refs[gi].shape)
        if gi in c_refs:
            c_ref, heads = c_refs[gi], cache_heads[gi]
            seq = c_ref.shape[1] // heads
            hw = c_ref.shape[2]
            for hl in range(acc.shape[1] // hw):
                hd = j * (acc.shape[1] // hw) + hl
                for bi in range(c_ref.shape[0]):
                    c_ref[bi, pl.ds(hd, seq, stride=heads), :] = acc[bi * seq:(bi + 1) * seq,
                                                                       hl * hw:(hl + 1) * hw]

    rows = h_ref.shape[1] // steps
    r0 = pl.multiple_of(j * rows, rows)
    h_ref[1 - slot, pl.ds(r0, rows), :] = modulated(xn_ref[pl.ds(r0, rows), :], shn_ref, scn_ref)


def _proj(x, seq, g_pre, ada_l, mod_row, weights, out_dtypes, steps, rope_flags=None,
          rope_tables=None, rope_half=0, out_scales=None, cache_slots=None, tm=512):
    t, d = x.shape
    n_groups = len(weights)
    n_tiles = t // tm
    rope_flags = tuple(rope_flags) if rope_flags is not None else (False,) * n_groups
    out_scales = tuple(out_scales) if out_scales is not None else (1.0,) * n_groups
    cache_slots = cache_slots or {}
    tiles_per_seq = max(seq // tm, 1)
    nxt = lambda i: jnp.minimum(i + 1, n_tiles - 1)
    aliased = [(gi, cs[2]) for gi, cs in sorted(cache_slots.items()) if cs[2] is not None]
    in_specs = [pl.BlockSpec(memory_space=pl.ANY) for _ in aliased]
    in_specs += [
        pl.BlockSpec((tm, d), lambda i, j: (0, 0)),
        pl.BlockSpec((tm, d), lambda i, j: (nxt(i), 0)),
        pl.BlockSpec((1, d), lambda i, j: (0, 0)),
        pl.BlockSpec((None, 1, d), lambda i, j: (mod_row(0), 0, 0)),
        pl.BlockSpec((None, 1, d), lambda i, j: (mod_row(0), 0, 1)),
        pl.BlockSpec((None, 1, d), lambda i, j: (mod_row(nxt(i)), 0, 0)),
        pl.BlockSpec((None, 1, d), lambda i, j: (mod_row(nxt(i)), 0, 1)),
    ]
    args = [prev for _, prev in aliased] + [x, x, g_pre.reshape(1, d), ada_l, ada_l, ada_l, ada_l]
    if any(rope_flags):
        for tab in rope_tables:
            in_specs.append(pl.BlockSpec((tm, LANES), lambda i, j: (i % tiles_per_seq, 0)))
            args.append(tab)
    out_specs, out_shapes = [], []
    for gi, ((w, layer, col0, n), dt) in enumerate(zip(weights, out_dtypes)):
        tn = n // steps
        in_specs.append(pl.BlockSpec(
            (None, d, tn), functools.partial(lambda i, j, ly, b0: (ly, 0, b0 + j), ly=layer, b0=col0 // tn)))
        args.append(w)
        if gi in cache_slots and not cache_slots[gi][3]:
            slot, n_slots = cache_slots[gi][:2]
            out_specs.append(_slot_spec(tm // seq, seq, tn, slot, lambda i, j: j))
            out_shapes.append(jax.ShapeDtypeStruct((t // seq, n_slots, seq, n), dt))
        else:
            out_specs.append(pl.BlockSpec((tm, tn), lambda i, j: (i, j)))
            out_shapes.append(jax.ShapeDtypeStruct((t, n), dt))
    merged = [gi for gi in sorted(cache_slots) if cache_slots[gi][3]]
    for gi in merged:
        slot, n_slots, _, heads = cache_slots[gi]
        n = weights[gi][3]
        out_specs.append(_slot_spec(tm // seq, seq * heads, n // heads, slot, lambda i, j: 0))
        out_shapes.append(jax.ShapeDtypeStruct((t // seq, n_slots, seq * heads, n // heads), F32))
    cache_out = lambda gi: n_groups + merged.index(gi) if gi in merged else gi
    return pl.pallas_call(
        functools.partial(_proj_kernel, n_groups=n_groups, rope_flags=rope_flags, rope_half=rope_half,
                          out_scales=out_scales, n_alias=len(aliased),
                          cache_heads={gi: cache_slots[gi][3] for gi in merged}, steps=steps),
        out_shape=out_shapes,
        grid=(n_tiles, steps),
        in_specs=in_specs,
        out_specs=out_specs,
        input_output_aliases={k: cache_out(gi) for k, (gi, _) in enumerate(aliased)},
        scratch_shapes=[pltpu.VMEM((2, tm, d), BF16)],
        compiler_params=_params(("arbitrary", "arbitrary")),
        name="proj",
    )(*args)


def _mla_q_kernel(*refs, rope, rope_half, q_scale):
    if rope:
        cq_ref, g_ref, w_ref, cos_ref, slo_ref, shi_ref, o_ref = refs
    else:
        cq_ref, g_ref, w_ref, o_ref = refs
    n = _rms(cq_ref[...], g_ref[...]).astype(BF16)
    q = jnp.dot(n, w_ref[...], preferred_element_type=F32) * q_scale
    if rope:
        cos, slo, shi = cos_ref[...], slo_ref[...], shi_ref[...]
        parts = []
        for s in range(0, q.shape[1], 2 * LANES):
            parts.append(q[:, s:s + LANES])
            parts.append(_rope(q[:, s + LANES:s + 2 * LANES], cos, slo, shi, rope_half))
        q = jnp.concatenate(parts, axis=1)
    o_ref[...] = q.astype(o_ref.dtype)


def _mla_q(cq, seq, g_q, w_uq_cat, q_scale, rope_tables=None, rope_half=0, tm=512):
    t, r = cq.shape
    n = w_uq_cat.shape[1]
    rope = rope_tables is not None
    tiles_per_seq = max(seq // tm, 1)
    in_specs = [
        pl.BlockSpec((tm, r), lambda i: (i, 0)),
        pl.BlockSpec((1, r), lambda i: (0, 0)),
        pl.BlockSpec((r, n), lambda i: (0, 0)),
    ]
    args = [cq, g_q.reshape(1, r), w_uq_cat]
    if rope:
        for tab in rope_tables:
            in_specs.append(pl.BlockSpec((tm, LANES), lambda i: (i % tiles_per_seq, 0)))
            args.append(tab)
    return pl.pallas_call(
        functools.partial(_mla_q_kernel, rope=rope, rope_half=rope_half, q_scale=q_scale),
        out_shape=jax.ShapeDtypeStruct((t, n), BF16),
        grid=(t // tm,),
        in_specs=in_specs,
        out_specs=pl.BlockSpec((tm, n), lambda i: (i, 0)),
        compiler_params=_params(("arbitrary",)),
        name="mla_q",
    )(*args)


def _mla_kv_kernel(*refs, normalise, rope, rope_half, emit_cache, n_alias, n_heads, nope, kpe_width):
    refs = list(refs[n_alias:])
    ckv_ref, kpe_ref = refs[:2]
    pos = 2
    if normalise:
        g_ref = refs[pos]
        pos += 1
    wk_ref, wv_ref = refs[pos:pos + 2]
    pos += 2
    if rope:
        cos_ref, slo_ref, shi_ref = refs[pos:pos + 3]
        pos += 3
    kcat_ref, v_ref = refs[pos:pos + 2]
    pos += 2
    ckv = ckv_ref[...]
    if normalise:
        ckv = _rms(ckv, g_ref[...])
    kpe = kpe_ref[...]
    kpe = jnp.where(lax.broadcasted_iota(jnp.int32, kpe.shape, 1) < kpe_width, kpe, 0.0)
    if emit_cache:
        ckv_out_ref, kpe_out_ref = refs[pos:pos + 2]
        ckv_out_ref[...] = ckv.reshape(ckv_out_ref.shape)
        kpe_out_ref[...] = kpe[:, :kpe_out_ref.shape[-1]].reshape(kpe_out_ref.shape)
    if rope:
        kpe = _rope(kpe, cos_ref[...], slo_ref[...], shi_ref[...], rope_half)
    cb = ckv.astype(BF16)
    kn = jnp.dot(cb, wk_ref[...], preferred_element_type=F32).astype(BF16)
    v_ref[...] = jnp.dot(cb, wv_ref[...], preferred_element_type=F32).astype(BF16)
    kpe_b = kpe.astype(BF16)
    for h in range(n_heads):
        kcat_ref[:, h * 2 * nope:h * 2 * nope + nope] = kn[:, h * nope:(h + 1) * nope]
        kcat_ref[:, h * 2 * nope + nope:(h + 1) * 2 * nope] = kpe_b


def _mla_kv(ckv_arr, ckv_spec, kpe_arr, kpe_spec, t, seq, w_uk, w_uv, g_kv=None, rope_tables=None,
            rope_half=0, cache=None, kpe_width=64, tm=512):
    lora, nk = w_uk.shape
    nv = w_uv.shape[1]
    nope = LANES
    n_heads = nk // nope
    normalise = g_kv is not None
    rope = rope_tables is not None
    tiles_per_seq = max(seq // tm, 1)
    aliased = [] if cache is None or cache[2] is None else [cache[2], cache[3]]
    in_specs = [pl.BlockSpec(memory_space=pl.ANY) for _ in aliased] + [ckv_spec, kpe_spec]
    args = aliased + [ckv_arr, kpe_arr]
    if normalise:
        in_specs.append(pl.BlockSpec((1, lora), lambda i: (0, 0)))
        args.append(g_kv.reshape(1, lora))
    in_specs += [pl.BlockSpec((lora, nk), lambda i: (0, 0)), pl.BlockSpec((lora, nv), lambda i: (0, 0))]
    args += [w_uk, w_uv]
    if rope:
        for tab in rope_tables:
            in_specs.append(pl.BlockSpec((tm, LANES), lambda i: (i % tiles_per_seq, 0)))
            args.append(tab)
    out_shapes = [jax.ShapeDtypeStruct((t, 2 * nk), BF16), jax.ShapeDtypeStruct((t, nv), BF16)]
    out_specs = [pl.BlockSpec((tm, 2 * nk), lambda i: (i, 0)), pl.BlockSpec((tm, nv), lambda i: (i, 0))]
    if cache is not None:
        slot, n_slots = cache[:2]
        out_shapes += [jax.ShapeDtypeStruct((t // seq, n_slots, seq, lora), F32),
                       jax.ShapeDtypeStruct((t // seq, n_slots, seq, kpe_width), F32)]
        out_specs += [_slot_spec(tm // seq, seq, lora, slot, lambda i: 0),
                      _slot_spec(tm // seq, seq, kpe_width, slot, lambda i: 0)]
    return pl.pallas_call(
        functools.partial(_mla_kv_kernel, normalise=normalise, rope=rope, rope_half=rope_half,
                          emit_cache=cache is not None, n_alias=len(aliased), n_heads=n_heads, nope=nope,
                          kpe_width=kpe_width),
        out_shape=out_shapes,
        grid=(t // tm,),
        in_specs=in_specs,
        out_specs=out_specs,
        input_output_aliases={k: 2 + k for k in range(len(aliased))},
        compiler_params=_params(("arbitrary",)),
        name="mla_kv",
    )(*args)


def _softmax_parts(s_list):
    m = functools.reduce(jnp.maximum, [jnp.max(s, axis=-1, keepdims=True) for s in s_list])
    p_list = [jnp.exp2(s - m) for s in s_list]
    l = functools.reduce(lambda a, b: a + b, [jnp.sum(p, axis=-1, keepdims=True) for p in p_list])
    return p_list, l


def _qk(q, k):
    return lax.dot_general(q, k, (((1,), (1,)), ((), ())), preferred_element_type=F32)


def _diff_lambda(lam_ref, lam_init):
    lp = lam_ref[...]
    return (jnp.exp(jnp.sum(lp[0:1] * lp[1:2], axis=-1, keepdims=True))
            - jnp.exp(jnp.sum(lp[2:3] * lp[3:4], axis=-1, keepdims=True)) + lam_init)


def _softmax_weights(s_ref, m_ref, lam):
    ps = [jnp.exp2(s_ref[mp] - m_ref[mp]) for mp in range(s_ref.shape[0])]
    ls = [jnp.sum(p, axis=-1, keepdims=True) for p in ps]
    w = ps[0].astype(BF16)
    if len(ps) == 2:
        w = w - (lam * ls[0] / ls[1]).astype(BF16) * ps[1].astype(BF16)
    return w, 1.0 / ls[0]


def _attn_kernel(*refs, diff, lam_init, hps, dq, dv):
    q_ref, k_ref, v_ref, g_ref = refs[:4]
    pos = 4
    if diff:
        lam_ref, gsub_ref = refs[pos:pos + 2]
        pos += 2
        lam = _diff_lambda(lam_ref, lam_init)
    o_ref = refs[pos]
    s_ref, w_ref = refs[pos + 1:pos + 3]
    n_maps = 2 if diff else 1
    d = dq // n_maps
    for hh in range(hps):
        for mp in range(n_maps):
            c0 = hh * dq + mp * d
            s_ref[hh * n_maps + mp] = _qk(q_ref[:, c0:c0 + d].astype(BF16), k_ref[:, c0:c0 + d].astype(BF16))
    inv_l = []
    for hh in range(hps):
        if not diff:
            (p,), l = _softmax_parts([s_ref[hh]])
            w_ref[hh] = p.astype(BF16)
            inv_l.append(1.0 / l)
        else:
            (p1,), l1 = _softmax_parts([s_ref[2 * hh]])
            (p2,), l2 = _softmax_parts([s_ref[2 * hh + 1]])
            w_ref[hh] = (p1 * (1.0 / l1) - p2 * (lam / l2)).astype(BF16)
    for hh in range(hps):
        cols = slice(hh * dv, (hh + 1) * dv)
        o = jnp.dot(w_ref[hh], v_ref[:, cols].astype(BF16), preferred_element_type=F32)
        if diff:
            o = _rms(o, gsub_ref[...]) * (1.0 - lam_init)
        else:
            o = o * inv_l[hh]
        o_ref[:, cols] = (o * _silu(g_ref[:, cols].astype(F32))).astype(o_ref.dtype)


def _attention(q, seg, gate, n_heads, dq, dv, tq, hps=1, diff=False, lam_params=None,
               g_sub=None, lam_init=0.0):
    b, sq, _ = q.shape
    k_arr, k_spec, v_arr, v_spec = seg
    sk = k_spec.block_shape[-2]
    n_maps = 2 if diff else 1
    in_specs = [pl.BlockSpec((None, tq, hps * dq), lambda bi, h, qi: (bi, qi, h)), k_spec, v_spec,
                pl.BlockSpec((None, tq, hps * dv), lambda bi, h, qi: (bi, qi, h))]
    args = [q, k_arr, v_arr, gate]
    if diff:
        in_specs += [pl.BlockSpec(lam_params.shape, lambda bi, h, qi: (0, 0)),
                     pl.BlockSpec((1, dv), lambda bi, h, qi: (0, 0))]
        args += [lam_params, g_sub.reshape(1, dv)]
    return pl.pallas_call(
        functools.partial(_attn_kernel, diff=diff, lam_init=lam_init, hps=hps, dq=dq, dv=dv),
        out_shape=jax.ShapeDtypeStruct((b, sq, n_heads * dv), BF16),
        grid=(b, n_heads // hps, sq // tq),
        in_specs=in_specs,
        out_specs=pl.BlockSpec((None, tq, hps * dv), lambda bi, h, qi: (bi, qi, h)),
        scratch_shapes=[pltpu.VMEM((hps * n_maps, tq, sk), F32), pltpu.VMEM((hps, tq, sk), BF16)],
        compiler_params=_params(("arbitrary", "arbitrary", "arbitrary")),
        name="attn_diff" if diff else "attn",
    )(*args)


def _seg3(arr, sk, width):
    return arr, pl.BlockSpec((None, sk, width), lambda bi, h, qi: (bi, 0, h))


def _seg4(arr, layer, sk, width):
    return arr, pl.BlockSpec((None, None, sk, width), lambda bi, h, qi: (bi, layer, 0, h))


def _attn_latent_kernel(*refs, diff, lam_init, tq, dq, dv):
    q_ref, kc_ref, vc_ref, k_ref, v_ref, g_ref = refs[:6]
    pos = 6
    lam = None
    if diff:
        lam_ref, gsub_ref = refs[pos:pos + 2]
        pos += 2
        lam = _diff_lambda(lam_ref, lam_init)
    o_ref = refs[pos]
    s_bufs = refs[pos + 1:pos + 3]
    m_bufs = refs[pos + 3:pos + 5]
    kcb_ref, vcb_ref = refs[pos + 5:pos + 7]
    n_maps = 2 if diff else 1
    d = dq // n_maps
    past = kc_ref.shape[0]
    n_blocks = q_ref.shape[0] // tq

    kcb_ref[...] = kc_ref[...].astype(BF16)
    vcb_ref[...] = vc_ref[...].astype(BF16)

    def scores(qi, slot):
        q = q_ref[qi * tq:(qi + 1) * tq, :]
        for mp in range(n_maps):
            qm = q[:, mp * d:(mp + 1) * d]
            s_c = _qk(qm, kcb_ref[:, mp * d:(mp + 1) * d])
            s_l = _qk(qm, k_ref[:, mp * d:(mp + 1) * d])
            s_bufs[slot][mp, :, 0:past] = s_c
            s_bufs[slot][mp, :, past:] = s_l
            m_bufs[slot][mp] = jnp.maximum(jnp.max(s_c, axis=-1, keepdims=True),
                                           jnp.max(s_l, axis=-1, keepdims=True))

    def finish(qi, slot):
        rows = slice(qi * tq, (qi + 1) * tq)
        w, inv_l = _softmax_weights(s_bufs[slot], m_bufs[slot], lam)
        o = (jnp.dot(w[:, :past], vcb_ref[...], preferred_element_type=F32)
             + jnp.dot(w[:, past:], v_ref[...], preferred_element_type=F32)) * inv_l
        if diff:
            o = _rms(o, gsub_ref[...]) * (1.0 - lam_init)
        o_ref[rows, :] = (o * _silu(g_ref[rows, :].astype(F32))).astype(o_ref.dtype)

    scores(0, 0)
    for t in range(n_blocks):
        if t + 1 < n_blocks:
            scores(t + 1, (t + 1) % 2)
        finish(t, t % 2)


def _attention_latent(q, kc, kc_spec, vc, vc_spec, k, v, gate, n_heads, dq, dv, tq=256, diff=False,
                      lam_params=None, g_sub=None, lam_init=0.0):
    b, s, _ = q.shape
    past = kc_spec.block_shape[-2]
    n_maps = 2 if diff else 1
    head = lambda w: pl.BlockSpec((None, s, w), lambda bi, h: (bi, 0, h))
    in_specs = [head(dq), kc_spec, vc_spec, head(dq), head(dv), head(dv)]
    args = [q, kc, vc, k, v, gate]
    if diff:
        in_specs += [pl.BlockSpec(lam_params.shape, lambda bi, h: (0, 0)),
                     pl.BlockSpec((1, dv), lambda bi, h: (0, 0))]
        args += [lam_params, g_sub.reshape(1, dv)]
    return pl.pallas_call(
        functools.partial(_attn_latent_kernel, diff=diff, lam_init=lam_init, tq=tq,
                          dq=dq, dv=dv),
        out_shape=jax.ShapeDtypeStruct((b, s, n_heads * dv), BF16),
        grid=(b, n_heads),
        in_specs=in_specs,
        out_specs=head(dv),
        scratch_shapes=[pltpu.VMEM((n_maps, tq, past + s), F32), pltpu.VMEM((n_maps, tq, past + s), F32),
                        pltpu.VMEM((n_maps, tq, 1), F32), pltpu.VMEM((n_maps, tq, 1), F32),
                        pltpu.VMEM((past, dq), BF16), pltpu.VMEM((past, dv), BF16)],
        compiler_params=_params(("arbitrary", "arbitrary")),
        name="attn_latent_diff" if diff else "attn_latent",
    )(*args)


NA_Q_ROWS = 4
NA_K_ROWS = WIN_R + NA_Q_ROWS
NA_K_PAIRS = NA_K_ROWS * GRID_W // LANES
RPB_W = 2 * WIN_C - 1


def _na_blocks(rows):
    assert 2 * GRID_W == LANES and rows >= NA_K_ROWS
    wr = min(WIN_R, rows)
    r = np.arange(rows)
    rs = np.clip(r - wr // 2, 0, rows - wr)
    starts, variants, keys = [], [], {}
    for t in range(rows // NA_Q_ROWS):
        r0 = t * NA_Q_ROWS
        ws = int(np.clip(r0 - wr // 2, 0, rows - NA_K_ROWS))
        sig = (r0 - ws,) + tuple(int(rs[r0 + dq] - ws) for dq in range(NA_Q_ROWS))
        assert all(0 <= f and f + wr <= NA_K_ROWS for f in sig[1:])
        variants.append(keys.setdefault(sig, len(keys)))
        starts.append(ws)
    sigs = sorted(keys, key=keys.get)
    idx = np.zeros((len(sigs), NA_Q_ROWS, NA_K_PAIRS, LANES), np.int32)
    kind = np.ones(idx.shape, np.int32)
    for vi, sig in enumerate(sigs):
        for dq in range(NA_Q_ROWS):
            for j in range(NA_K_ROWS):
                lanes = slice((j % 2) * GRID_W, (j % 2) * GRID_W + RPB_W)
                if sig[1 + dq] <= j < sig[1 + dq] + wr:
                    dr = j - (sig[0] + dq)
                    idx[vi, dq, j // 2, lanes] = (dr + WIN_R - 1) * RPB_W + np.arange(RPB_W)
                    kind[vi, dq, j // 2, lanes] = 0
                else:
                    kind[vi, dq, j // 2, (j % 2) * GRID_W:(j % 2 + 1) * GRID_W] = 2
    return starts, variants, idx, kind


def _na_kernel(q_ref, k_ref, v_ref, kc_ref, vc_ref, rows_ref, g_ref, o_ref, bias_ref, *, starts, variants):
    nq = NA_Q_ROWS * GRID_W
    nk = NA_K_ROWS * GRID_W

    @pl.when(pl.program_id(1) == 0)
    def _():
        ci = lax.broadcasted_iota(jnp.int32, (GRID_W, LANES), 0)
        kc = lax.broadcasted_iota(jnp.int32, (GRID_W, LANES), 1) & (GRID_W - 1)
        cs = jnp.clip(ci - WIN_C // 2, 0, GRID_W - WIN_C)
        col_mask = jnp.where((kc >= cs) & (kc < cs + WIN_C), 0.0, NEG_BIG)
        for vi in range(bias_ref.shape[0]):
            for dq in range(NA_Q_ROWS):
                vecs = rows_ref[vi, dq]
                for jp in range(NA_K_PAIRS):
                    w = jnp.broadcast_to(vecs[jp:jp + 1, :], (GRID_W, LANES))
                    tile = pltpu.roll(w, LANES - (WIN_C - 1), 1, stride=1, stride_axis=0)
                    bias_ref[vi, dq * GRID_W:(dq + 1) * GRID_W, jp * LANES:(jp + 1) * LANES] = (
                        (tile + col_mask) * LOG2E)

    kc = kc_ref[...].astype(BF16)
    vc = vc_ref[...].astype(BF16)

    def body(t, carry):
        start = jnp.int32(starts[0])
        var = jnp.int32(variants[0])
        for ti in range(1, len(starts)):
            start = jnp.where(t == ti, jnp.int32(starts[ti]), start)
            var = jnp.where(t == ti, jnp.int32(variants[ti]), var)
        q0 = pl.multiple_of(t * nq, nq)
        k0 = pl.multiple_of(start * GRID_W, GRID_W)
        q = q_ref[pl.ds(q0, nq), :]
        kw = k_ref[pl.ds(k0, nk), :]
        vw = v_ref[pl.ds(k0, nk), :]
        s_w = _qk(q, kw) + bias_ref[var]
        s_c = _qk(q, kc)
        (p_w, p_c), l = _softmax_parts([s_w, s_c])
        o = (jnp.dot(p_w.astype(BF16), vw, preferred_element_type=F32)
             + jnp.dot(p_c.astype(BF16), vc, preferred_element_type=F32)) / l
        g = g_ref[pl.ds(q0, nq), :].astype(F32)
        o_ref[pl.ds(q0, nq), :] = (o * _silu(g)).astype(o_ref.dtype)
        return carry

    lax.fori_loop(0, len(starts), body, 0, unroll=8)


def _na_latent(q, k, v, cache_k, cache_v, layer, bias_rows, gate, n_heads, d, starts, variants):
    b, s, _ = q.shape
    past = cache_k.shape[2]
    n_var = bias_rows.shape[1]
    head = pl.BlockSpec((None, s, d), lambda h, bi: (bi, 0, h))
    ctx = pl.BlockSpec((None, None, past, d), lambda h, bi: (bi, layer, 0, h))
    return pl.pallas_call(
        functools.partial(_na_kernel, starts=tuple(starts), variants=tuple(variants)),
        out_shape=jax.ShapeDtypeStruct((b, s, n_heads * d), BF16),
        grid=(n_heads, b),
        in_specs=[head, head, head, ctx, ctx,
                  pl.BlockSpec((None,) + bias_rows.shape[1:], lambda h, bi: (h, 0, 0, 0, 0)),
                  head],
        out_specs=head,
        scratch_shapes=[pltpu.VMEM((n_var, NA_Q_ROWS * GRID_W, NA_K_ROWS * GRID_W), F32)],
        compiler_params=_params(("arbitrary", "arbitrary")),
        name="na_latent",
    )(q, k, v, cache_k, cache_v, bias_rows, gate)


def _merge_kernel(*refs, n_in):
    o_refs = refs[:n_in]
    w_ref, x_ref, gate_ref, g_ref, out_ref = refs[n_in:]
    y = None
    row = 0
    for o_ref in o_refs:
        width = o_ref.shape[1]
        part = jnp.dot(o_ref[...], w_ref[row:row + width, :], preferred_element_type=F32)
        y = part if y is None else y + part
        row += width
    out_ref[...] = x_ref[...] + gate_ref[...] * _rms(y, g_ref[...])


def _merge(o_list, w_out, layer, x, ada_l, mod_row, g_post, tm=512):
    t, d = x.shape
    in_specs = [pl.BlockSpec((tm, o.shape[1]), lambda i: (i, 0)) for o in o_list]
    in_specs += [
        pl.BlockSpec((None,) + w_out.shape[1:], lambda i: (layer, 0, 0)),
        pl.BlockSpec((tm, d), lambda i: (i, 0)),
        pl.BlockSpec((None, 1, d), lambda i: (mod_row(i), 0, 2)),
        pl.BlockSpec((1, d), lambda i: (0, 0)),
    ]
    return pl.pallas_call(
        functools.partial(_merge_kernel, n_in=len(o_list)),
        out_shape=jax.ShapeDtypeStruct((t, d), F32),
        grid=(t // tm,),
        in_specs=in_specs,
        out_specs=pl.BlockSpec((tm, d), lambda i: (i, 0)),
        compiler_params=_params(("arbitrary",)),
        name="merge",
    )(*o_list, w_out, x, ada_l, g_post.reshape(1, d))


def _rope_tables(s, n):
    a = n // 2
    half = a // 2
    t = jnp.arange(s)
    inv = ROPE_BASE ** (-jnp.arange(half, dtype=F32) / half)
    ang_r = (t // GRID_W).astype(F32)[:, None] * inv[None, :]
    ang_c = (t % GRID_W).astype(F32)[:, None] * inv[None, :]
    cos = jnp.concatenate([jnp.cos(ang_r)] * 2 + [jnp.cos(ang_c)] * 2, axis=-1)
    sin = jnp.concatenate([jnp.sin(ang_r)] * 2 + [jnp.sin(ang_c)] * 2, axis=-1)
    low = (np.arange(n) % a) < half
    sin_lo = jnp.where(low[None, :], -sin, 0.0)
    sin_hi = jnp.where(low[None, :], 0.0, sin)
    pad = LANES - n
    if pad:
        cos = jnp.pad(cos, ((0, 0), (0, pad)), constant_values=1.0)
        sin_lo = jnp.pad(sin_lo, ((0, 0), (0, pad)))
        sin_hi = jnp.pad(sin_hi, ((0, 0), (0, pad)))
    return (cos, sin_lo, sin_hi), half


def kernel(x_prompt, x_sample, cache_na_k, cache_na_v, cache_mla_ckv, cache_mla_kpe, cache_diff_k, cache_diff_v, c, c_ctx, w_ada, b_ada, g_pre, g_post, w_in_even, w_out_even, na_rpb, mla_g_q, mla_w_uq, mla_g_kv, mla_w_ukv, w_in_odd, w_out_odd, diff_lambda, diff_g):
    bp, sp, d = x_prompt.shape
    bs, ss, _ = x_sample.shape
    depth = w_ada.shape[0]
    n_even, n_odd = w_in_even.shape[0], w_in_odd.shape[0]
    past = cache_na_k.shape[2]
    na_heads, na_dim = cache_na_k.shape[3:]
    na_w = na_heads * na_dim
    kv_lora = cache_mla_ckv.shape[3]
    rope_w = cache_mla_kpe.shape[3]
    q_lora = mla_g_q.shape[1]
    diff_heads = cache_diff_k.shape[3]
    diff_w = diff_heads * cache_diff_k.shape[4]
    diff_d = diff_w // (2 * diff_heads)
    mla_w = w_out_even.shape[1] - na_w
    mla_v = mla_nope = LANES
    mla_heads = mla_w // mla_v
    na_scale = na_dim ** -0.5
    mla_scale = (mla_nope + rope_w) ** -0.5
    diff_scale = diff_d ** -0.5
    tp, ts = bp * sp, bs * ss
    tm = 512
    even_scales = (na_scale * LOG2E, 1.0, 1.0, 1.0, 1.0, 1.0, 1.0)
    odd_scales = (diff_scale * LOG2E, 1.0, 1.0, 1.0)

    ctx_row = bs
    cond = jnp.concatenate([c, c_ctx[None, :], jnp.zeros((16 - bs - 1, d), F32)], axis=0)
    ada = _ada(cond, w_ada, b_ada)
    row_p = lambda i: ctx_row
    row_s = lambda i: (i * tm) // ss

    rope_diff, half_diff = _rope_tables(ss, diff_d)
    rope_mla, half_mla = _rope_tables(ss, rope_w)

    starts, variants, rows_idx, rows_kind = _na_blocks(ss // GRID_W)

    xp = x_prompt.reshape(tp, d)
    xs = x_sample.reshape(ts, d)
    ck_na = cache_na_k.reshape(bs, -1, past, na_w)
    cv_na = cache_na_v.reshape(bs, -1, past, na_w)
    ck_diff = cache_diff_k.reshape(bs, -1, past, diff_w)
    cv_diff = cache_diff_v.reshape(bs, -1, past, diff_w)
    cache_kpe_pad = jnp.pad(cache_mla_kpe, ((0, 0), (0, 0), (0, 0), (0, LANES - rope_w)))

    kv_start = 4 * na_w + q_lora
    kv_end = kv_start + kv_lora + rope_w
    n_in = w_in_even.shape[2]
    w_even = jnp.pad(w_in_even, ((0, 0), (0, 0), (0, -n_in % (4 * LANES)))).astype(BF16)
    w_gb = w_in_even[:, :, kv_end:].astype(BF16)
    w_odd = w_in_odd.astype(BF16)
    w_out_even_b = w_out_even.astype(BF16)
    w_out_odd_b = w_out_odd.astype(BF16)

    new_na_k = new_na_v = new_ckv = new_kpe = new_dk = new_dv = None
    for l in range(depth):
        i = l // 2
        ada_l = ada[l].reshape(16, 1, 3 * d)
        if l % 2 == 0:
            weights = [(w_even, i, k * na_w, na_w) for k in range(4)]
            weights += [(w_even, i, 4 * na_w, q_lora), (w_even, i, kv_start, 512), (w_gb, i, 0, mla_w)]
            w_uq = mla_w_uq[i].reshape(q_lora, mla_heads, mla_nope + rope_w)
            w_uq_cat = jnp.pad(w_uq, ((0, 0), (0, 0), (0, 2 * LANES - mla_nope - rope_w)))
            w_uq_cat = w_uq_cat.reshape(q_lora, mla_heads * 2 * LANES).astype(BF16)
            w_ukv = mla_w_ukv[i].reshape(kv_lora, mla_heads, mla_nope + mla_v)
            w_uk = w_ukv[:, :, :mla_nope].reshape(kv_lora, mla_heads * mla_nope).astype(BF16)
            w_uv = w_ukv[:, :, mla_nope:].reshape(kv_lora, mla_heads * mla_v).astype(BF16)
            rpb_flat = na_rpb[i].reshape(na_heads, -1)
            bias_rows = jnp.where(rows_kind[None] == 0, rpb_flat[:, rows_idx],
                                  jnp.where(rows_kind[None] == 1, 0.0, NEG_BIG))

            qa, ka, va, ga, cq, ckvkpe, gb, new_na_k, new_na_v = _proj(
                xp, sp, g_pre[l], ada_l, row_p, weights, [BF16, BF16, BF16, BF16, F32, F32, BF16], steps=4,
                out_scales=even_scales, cache_slots={1: (i, n_even, new_na_k, na_heads),
                                                     2: (i, n_even, new_na_v, na_heads)})
            q_cat = _mla_q(cq, sp, mla_g_q[i], w_uq_cat, mla_scale * LOG2E)
            k_cat, v_mla, new_ckv, new_kpe = _mla_kv(
                ckvkpe, pl.BlockSpec((tm, kv_lora), lambda i_: (i_, 0)),
                ckvkpe, pl.BlockSpec((tm, LANES), lambda i_: (i_, kv_lora // LANES)),
                tp, sp, w_uk, w_uv, g_kv=mla_g_kv[i], cache=(i, n_even, new_ckv, new_kpe), kpe_width=rope_w)
            r3 = lambda a: a.reshape(bp, sp, -1)
            oa = _attention(r3(qa), _seg3(r3(ka), sp, na_w) + _seg3(r3(va), sp, na_w), r3(ga),
                            na_heads, na_dim, na_dim, tq=sp, hps=na_heads)
            ob = _attention(r3(q_cat), _seg3(r3(k_cat), sp, mla_heads * 2 * LANES)
                            + _seg3(r3(v_mla), sp, mla_w),
                            r3(gb), mla_heads, 2 * LANES, mla_v, tq=sp, hps=mla_heads)
            xp = _merge([oa.reshape(tp, -1), ob.reshape(tp, -1)], w_out_even_b, i, xp, ada_l, row_p, g_post[l])

            qa, ka, va, ga, cq, ckvkpe, gb = _proj(
                xs, ss, g_pre[l], ada_l, row_s, weights, [BF16, BF16, BF16, BF16, F32, F32, BF16], steps=4,
                out_scales=even_scales)
            q_cat = _mla_q(cq, ss, mla_g_q[i], w_uq_cat, mla_scale * LOG2E, rope_tables=rope_mla,
                           rope_half=half_mla)
            k_cat, v_mla = _mla_kv(
                ckvkpe, pl.BlockSpec((tm, kv_lora), lambda i_: (i_, 0)),
                ckvkpe, pl.BlockSpec((tm, LANES), lambda i_: (i_, kv_lora // LANES)),
                ts, ss, w_uk, w_uv, g_kv=mla_g_kv[i], rope_tables=rope_mla, rope_half=half_mla,
                kpe_width=rope_w)
            tiles_past = past // tm
            kc_cat, vc_mla = _mla_kv(
                cache_mla_ckv, pl.BlockSpec((None, None, tm, kv_lora),
                                            lambda i_: (i_ // tiles_past, i, i_ % tiles_past, 0)),
                cache_kpe_pad, pl.BlockSpec((None, None, tm, LANES),
                                            lambda i_: (i_ // tiles_past, i, i_ % tiles_past, 0)),
                bs * past, past, w_uk, w_uv, kpe_width=rope_w)
            r3 = lambda a: a.reshape(bs, ss, -1)
            oa = _na_latent(r3(qa), r3(ka), r3(va), ck_na, cv_na, i, bias_rows, r3(ga), na_heads, na_dim,
                            starts, variants)
            c3 = lambda a: a.reshape(bs, past, -1)
            ob = _attention_latent(
                r3(q_cat),
                c3(kc_cat), pl.BlockSpec((None, past, 2 * LANES), lambda bi, h: (bi, 0, h)),
                c3(vc_mla), pl.BlockSpec((None, past, mla_v), lambda bi, h: (bi, 0, h)),
                r3(k_cat), r3(v_mla), r3(gb), mla_heads, 2 * LANES, mla_v)
            xs = _merge([oa.reshape(ts, -1), ob.reshape(ts, -1)], w_out_even_b, i, xs, ada_l, row_s, g_post[l])
        else:
            lam_init = 0.8 - 0.6 * math.exp(-0.3 * l)
            weights = [(w_odd, i, k * diff_w, diff_w) for k in range(4)]
            dh = 2 * diff_d

            q, new_dk, new_dv, g = _proj(
                xp, sp, g_pre[l], ada_l, row_p, weights, [BF16, F32, F32, BF16], steps=4,
                out_scales=odd_scales, cache_slots={1: (i, n_odd, new_dk, 0), 2: (i, n_odd, new_dv, 0)})
            r3 = lambda a: a.reshape(bp, sp, -1)
            o = _attention(r3(q), _seg4(new_dk, i, sp, diff_w) + _seg4(new_dv, i, sp, diff_w), r3(g),
                           diff_heads, dh, dh, tq=sp, hps=diff_heads, diff=True,
                           lam_params=diff_lambda[i], g_sub=diff_g[i], lam_init=lam_init)
            xp = _merge([o.reshape(tp, -1)], w_out_odd_b, i, xp, ada_l, row_p, g_post[l])

            q, k, v, g = _proj(xs, ss, g_pre[l], ada_l, row_s, weights, [BF16, BF16, BF16, BF16], steps=4,
                               rope_flags=(True, True, False, False), rope_tables=rope_diff,
                               rope_half=half_diff, out_scales=odd_scales)
            r3 = lambda a: a.reshape(bs, ss, -1)
            ctx_spec = pl.BlockSpec((None, None, past, dh), lambda bi, h: (bi, i, 0, h))
            o = _attention_latent(
                r3(q), ck_diff, ctx_spec, cv_diff, ctx_spec, r3(k), r3(v), r3(g), diff_heads, dh, dh,
                diff=True, lam_params=diff_lambda[i], g_sub=diff_g[i], lam_init=lam_init)
            xs = _merge([o.reshape(ts, -1)], w_out_odd_b, i, xs, ada_l, row_s, g_post[l])

    return (xp.reshape(bp, sp, d), xs.reshape(bs, ss, d),
            new_na_k.reshape(bp, n_even, sp, na_heads, na_dim),
            new_na_v.reshape(bp, n_even, sp, na_heads, na_dim),
            new_ckv, new_kpe,
            new_dk.reshape(bp, n_odd, sp, diff_heads, 2 * diff_d),
            new_dv.reshape(bp, n_odd, sp, diff_heads, 2 * diff_d))
```
